```python
import math
import jax, jax.numpy as jnp
from jax import lax
import numpy as np

D_MODEL = 4096
BATCH = 8
SEQ = 2048
DEPTH = 4

GLA_HEADS = 8
GLA_DK = 128
GLA_DV = 192
GLA_RANK = 16
GLA_TAU = 16.0
GLA_CHUNK = 64
SG_GROUPS = 8
SG_GROUP_WIDTH = 128
SG_CHUNK = 128
DIFF_HEADS = 6
DIFF_DK = 128
DIFF_DV = 256
ATTN_BLOCK = 128
D_FF = 6144
CONV_WIDTH = 3
N_BRANCH = 3
EPS = 1e-6

GLA_QK = GLA_HEADS * GLA_DK
GLA_V = GLA_HEADS * GLA_DV
SG_W = SG_GROUPS * SG_GROUP_WIDTH
DIFF_QK = DIFF_HEADS * 2 * DIFF_DK
DIFF_V = DIFF_HEADS * DIFF_DV
SPLIT_SIZES = (GLA_QK, GLA_QK, GLA_V, GLA_V, GLA_RANK, SG_W, SG_W, DIFF_QK, DIFF_QK, DIFF_V, N_BRANCH * D_MODEL)
IN_COLS = sum(SPLIT_SIZES)

kernel_name = "hybrid_gla_gmlp_diffattn_convffn"


def rmsnorm(x, g):
    xf = x.astype(jnp.float32)
    y = xf * lax.rsqrt(jnp.mean(xf * xf, axis=-1, keepdims=True) + EPS)
    return (y * g.astype(jnp.float32)).astype(x.dtype)


def layernorm(x, g, b):
    xf = x.astype(jnp.float32)
    mu = jnp.mean(xf, axis=-1, keepdims=True)
    xc = xf - mu
    var = jnp.mean(xc * xc, axis=-1, keepdims=True)
    return (xc * lax.rsqrt(var + EPS) * g.astype(jnp.float32) + b.astype(jnp.float32)).astype(x.dtype)


def alibi_slopes(n):
    def pow2(m):
        start = 2.0 ** (-8.0 / m)
        return [start ** (i + 1) for i in range(m)]
    if math.log2(n).is_integer():
        s = pow2(n)
    else:
        p = 2 ** int(math.floor(math.log2(n)))
        s = pow2(p) + pow2(2 * p)[0::2][: n - p]
    return jnp.asarray(np.array(s, dtype=np.float32))


def gla_chunked(q, k, v, log_a):
    B, S, H, DK = q.shape
    DV = v.shape[-1]
    C = GLA_CHUNK
    N = S // C

    def to_chunks(t):
        return t.reshape(B, N, C, H, t.shape[-1]).transpose(1, 0, 3, 2, 4).astype(jnp.float32)

    qc = to_chunks(q) * (DK ** -0.5)
    kc = to_chunks(k)
    vc = to_chunks(v)
    bc = jnp.cumsum(to_chunks(log_a), axis=3)
    causal = jnp.tril(jnp.ones((C, C), dtype=bool))[:, :, None]

    def step(state, inp):
        q_, k_, v_, b_ = inp
        diff = b_[:, :, :, None, :] - b_[:, :, None, :, :]
        decay = jnp.exp(jnp.where(causal, diff, -jnp.inf))
        attn = jnp.einsum('bhtd,bhsd,bhtsd->bhts', q_, k_, decay)
        o = jnp.einsum('bhts,bhsv->bhtv', attn, v_) + jnp.einsum('bhtd,bhdv->bhtv', q_ * jnp.exp(b_), state)
        b_last = b_[:, :, -1:, :]
        k_dec = k_ * jnp.exp(b_last - b_)
        state = state * jnp.exp(b_last[:, :, 0, :])[..., None] + jnp.einsum('bhsd,bhsv->bhdv', k_dec, v_)
        return state, o

    s0 = jnp.zeros((B, H, DK, DV), jnp.float32)
    _, o = lax.scan(step, s0, (qc, kc, vc, bc))
    return o.transpose(1, 0, 3, 2, 4).reshape(B, S, H, DV)


def spatial_gating(u, v, ln_g, ln_b, w_s, b_s):
    B, S, _ = u.shape
    N = S // SG_CHUNK
    u = jax.nn.gelu(u)
    v = layernorm(jax.nn.gelu(v), ln_g, ln_b)
    vc = v.reshape(B, N, SG_CHUNK, SG_GROUPS, SG_GROUP_WIDTH)
    w = w_s * jnp.tril(jnp.ones((SG_CHUNK, SG_CHUNK), dtype=w_s.dtype))
    f = jnp.einsum('gts,bnsgc->bntgc', w, vc) + b_s.T[None, None, :, :, None]
    return u * f.reshape(B, S, SG_W)


def diff_attention(q, k, v, lam, slopes):
    B, S, H, _, DK = q.shape
    DV = v.shape[-1]
    NB = S // ATTN_BLOCK
    qb = q.reshape(B, NB, ATTN_BLOCK, H, 2, DK).transpose(1, 0, 2, 3, 4, 5)
    kf = k.astype(jnp.float32)
    vf = v.astype(jnp.float32)
    key_pos = jnp.arange(S)
    scale = DK ** -0.5

    def block(args):
        q_blk, i = args
        q_pos = i * ATTN_BLOCK + jnp.arange(ATTN_BLOCK)
        dist = (q_pos[:, None] - key_pos[None, :]).astype(jnp.float32)
        bias = jnp.where(dist[None] >= 0, -slopes[:, None, None] * dist[None], -jnp.inf)
        s = jnp.einsum('bqhmd,bkhmd->bhmqk', q_blk.astype(jnp.float32), kf) * scale + bias[None, :, None]
        p = jax.nn.softmax(s, axis=-1)
        a = p[:, :, 0] - lam * p[:, :, 1]
        return jnp.einsum('bhqk,bkhv->bqhv', a, vf)

    o = lax.map(block, (qb, jnp.arange(NB)))
    return o.transpose(1, 0, 2, 3, 4).reshape(B, S, H, DV)


def causal_dwconv(x, w, b):
    S = x.shape[1]
    xp = jnp.pad(x, ((0, 0), (CONV_WIDTH - 1, 0), (0, 0)))
    y = b
    for j in range(CONV_WIDTH):
        y = y + w[j] * xp[:, j:j + S, :]
    return y


def setup_inputs(seed: int = 0) -> dict:
    key = jax.random.key(seed)
    ks = jax.random.split(key, 32)
    L, D = DEPTH, D_MODEL
    f32 = jnp.float32

    def nrm(k, shape, scale):
        return jax.random.normal(k, shape, f32) * scale

    def gain(k, shape):
        return 1.0 + 0.05 * jax.random.normal(k, shape, f32)

    return {
        "x": jax.random.normal(ks[0], (BATCH, SEQ, D), f32),
        "g_pre_mix": gain(ks[1], (L, D)),
        "w_in": nrm(ks[2], (L, D, IN_COLS), D ** -0.5),
        "w_gla_lr": nrm(ks[3], (L, GLA_RANK, GLA_QK), GLA_RANK ** -0.5),
        "b_gla_lr": nrm(ks[4], (L, GLA_QK), 0.1),
        "gla_norm_g": gain(ks[5], (L, GLA_DV)),
        "sg_ln_g": gain(ks[6], (L, SG_W)),
        "sg_ln_b": nrm(ks[7], (L, SG_W), 0.02),
        "sg_w_s": nrm(ks[8], (L, SG_GROUPS, SG_CHUNK, SG_CHUNK), SG_CHUNK ** -0.5),
        "sg_b_s": gain(ks[9], (L, SG_GROUPS, SG_CHUNK)),
        "diff_lambda_q1": nrm(ks[10], (L, DIFF_DK), 0.1),
        "diff_lambda_k1": nrm(ks[11], (L, DIFF_DK), 0.1),
        "diff_lambda_q2": nrm(ks[12], (L, DIFF_DK), 0.1),
        "diff_lambda_k2": nrm(ks[13], (L, DIFF_DK), 0.1),
        "diff_norm_g": gain(ks[14], (L, DIFF_DV)),
        "w_br_gla": nrm(ks[15], (L, GLA_V, D), GLA_V ** -0.5),
        "w_br_sg": nrm(ks[16], (L, SG_W, D), SG_W ** -0.5),
        "w_br_diff": nrm(ks[17], (L, DIFF_V, D), DIFF_V ** -0.5),
        "w_o": nrm(ks[18], (L, D, D), D ** -0.5),
        "g_post_mix": gain(ks[19], (L, D)),
        "g_pre_ffn": gain(ks[20], (L, D)),
        "w_up": nrm(ks[21], (L, D, 2 * D_FF), D ** -0.5),
        "conv_w": nrm(ks[22], (L, CONV_WIDTH, 2 * D_FF), CONV_WIDTH ** -0.5),
        "conv_b": nrm(ks[23], (L, 2 * D_FF), 0.02),
        "w_down": nrm(ks[24], (L, D_FF, D), D_FF ** -0.5),
        "g_post_ffn": gain(ks[25], (L, D)),
    }


def reference(x, g_pre_mix, w_in, w_gla_lr, b_gla_lr, gla_norm_g, sg_ln_g, sg_ln_b, sg_w_s, sg_b_s,
              diff_lambda_q1, diff_lambda_k1, diff_lambda_q2, diff_lambda_k2, diff_norm_g,
              w_br_gla, w_br_sg, w_br_diff, w_o, g_post_mix, g_pre_ffn, w_up, conv_w, conv_b, w_down, g_post_ffn):
    B, S, D = x.shape
    offsets = np.cumsum(np.array(SPLIT_SIZES))[:-1].tolist()
    slopes = alibi_slopes(DIFF_HEADS)
    for l in range(DEPTH):
        h = rmsnorm(x, g_pre_mix[l])
        proj = h @ w_in[l]
        (gq, gk, gv, gg, glr, su, sv, dq, dk, dv, gates) = jnp.split(proj, offsets, axis=-1)

        gate_logit = (glr @ w_gla_lr[l] + b_gla_lr[l]).astype(jnp.float32)
        log_a = jax.nn.log_sigmoid(gate_logit) / GLA_TAU
        o_gla = gla_chunked(gq.reshape(B, S, GLA_HEADS, GLA_DK), gk.reshape(B, S, GLA_HEADS, GLA_DK),
                            gv.reshape(B, S, GLA_HEADS, GLA_DV), log_a.reshape(B, S, GLA_HEADS, GLA_DK))
        o_gla = rmsnorm(o_gla, gla_norm_g[l]).astype(x.dtype) * jax.nn.silu(gg.reshape(B, S, GLA_HEADS, GLA_DV))
        o_gla = o_gla.reshape(B, S, GLA_V)

        o_sg = spatial_gating(su, sv, sg_ln_g[l], sg_ln_b[l], sg_w_s[l], sg_b_s[l])

        lam_init = 0.8 - 0.6 * math.exp(-0.3 * l)
        lam = (jnp.exp(jnp.sum(diff_lambda_q1[l].astype(jnp.float32) * diff_lambda_k1[l].astype(jnp.float32)))
               - jnp.exp(jnp.sum(diff_lambda_q2[l].astype(jnp.float32) * diff_lambda_k2[l].astype(jnp.float32)))
               + lam_init)
        o_diff = diff_attention(dq.reshape(B, S, DIFF_HEADS, 2, DIFF_DK), dk.reshape(B, S, DIFF_HEADS, 2, DIFF_DK),
                                dv.reshape(B, S, DIFF_HEADS, DIFF_DV), lam, slopes)
        o_diff = (rmsnorm(o_diff, diff_norm_g[l]) * (1.0 - lam_init)).astype(x.dtype).reshape(B, S, DIFF_V)

        gate = jax.nn.sigmoid(gates.reshape(B, S, N_BRANCH, D))
        merged = (gate[:, :, 0] * (o_gla @ w_br_gla[l])
                  + gate[:, :, 1] * (o_sg @ w_br_sg[l])
                  + gate[:, :, 2] * (o_diff @ w_br_diff[l]))
        x = x + rmsnorm(merged @ w_o[l], g_post_mix[l])

        h = rmsnorm(x, g_pre_ffn[l])
        up = causal_dwconv(h @ w_up[l], conv_w[l], conv_b[l])
        a, u = jnp.split(up, 2, axis=-1)
        f = (jax.nn.gelu(a, approximate=True) * u) @ w_down[l]
        x = x + rmsnorm(f, g_post_ffn[l])
    return x
```

```python
import functools
import math

import jax
import jax.numpy as jnp
from jax import lax
from jax.experimental import pallas as pl
from jax.experimental.pallas import tpu as pltpu

D_MODEL = 4096
DEPTH = 4
GLA_HEADS, GLA_DK, GLA_DV, GLA_RANK, GLA_TAU = 8, 128, 192, 16, 16.0
SG_GROUPS, SG_GROUP_WIDTH, SG_CHUNK = 8, 128, 128
DIFF_HEADS, DIFF_DK, DIFF_DV = 6, 128, 256
D_FF = 6144
CONV_WIDTH = 3
N_BRANCH = 3
EPS = 1e-6

GLA_QK = GLA_HEADS * GLA_DK
GLA_V = GLA_HEADS * GLA_DV
SG_W = SG_GROUPS * SG_GROUP_WIDTH
DIFF_QK = DIFF_HEADS * 2 * DIFF_DK
DIFF_V = DIFF_HEADS * DIFF_DV
GATES_W = N_BRANCH * D_MODEL

_R_GQ = 0
_R_GK = _R_GQ + GLA_QK
_R_GV = _R_GK + GLA_QK
_R_GG = _R_GV + GLA_V
_R_LR = _R_GG + GLA_V
_R_SU = _R_LR + GLA_RANK
_R_SV = _R_SU + SG_W
_R_DQ = _R_SV + SG_W
_R_DK = _R_DQ + DIFF_QK
_R_DV = _R_DK + DIFF_QK
_R_GATES = _R_DV + DIFF_V
IN_COLS = _R_GATES + GATES_W

GLA_MAIN = 128
GLA_TAIL = GLA_DV - GLA_MAIN
GLA_MAIN_W = GLA_HEADS * GLA_MAIN
GLA_TAIL_W = GLA_HEADS * GLA_TAIL

_P_GATES = 0
_P_GQ = _P_GATES + GATES_W
_P_GK = _P_GQ + GLA_QK
_P_GVM = _P_GK + GLA_QK
_P_GVT = _P_GVM + GLA_MAIN_W
_P_GGM = _P_GVT + GLA_TAIL_W
_P_GGT = _P_GGM + GLA_MAIN_W
_P_SU = _P_GGT + GLA_TAIL_W
_P_SV = _P_SU + SG_W
_P_DQ = _P_SV + SG_W
_P_DK = _P_DQ + DIFF_QK
_P_DV = _P_DK + DIFF_QK
_P_END = _P_DV + DIFF_V

V7X_LANES = 128
V7X_VMEM_BYTES = 64 * 1024 * 1024
VMEM_LIMIT_CAP = V7X_VMEM_BYTES - 8 * 1024 * 1024

PROJ_TN = 1024
PROJ_COLS = -(-_P_END // PROJ_TN) * PROJ_TN

GLA_CHUNK = 128
GLA_ROWS = 512
GLA_SAFE_RANGE = 80.0
ATTN_TQ = 256
ATTN_TK = 256
SG_ROWS = 512
CONV_HALO = 16
NEG_BIG = -1e30


def _vmem_limit(nbytes):
    return int(min(VMEM_LIMIT_CAP, max(32 * 1024 * 1024, nbytes)))


def _cparams(sem, nbytes):
    return pltpu.CompilerParams(dimension_semantics=sem, vmem_limit_bytes=_vmem_limit(nbytes))


def _mm_kernel(x_ref, w_ref, o_ref):
    o_ref[...] = jnp.dot(x_ref[...], w_ref[...], preferred_element_type=jnp.float32).astype(o_ref.dtype)


def _matmul(x, w, tm, tn, out_dtype, name):
    m, k = x.shape
    n = w.shape[1]
    osz = jnp.dtype(out_dtype).itemsize
    need = 2 * (tm * k * 2 + k * tn * 2 + tm * tn * osz) + tm * tn * 4 + (4 << 20)
    return pl.pallas_call(
        _mm_kernel,
        grid=(m // tm, n // tn),
        in_specs=[pl.BlockSpec((tm, k), lambda i, j: (i, 0)),
                  pl.BlockSpec((k, tn), lambda i, j: (0, j))],
        out_specs=pl.BlockSpec((tm, tn), lambda i, j: (i, j)),
        out_shape=jax.ShapeDtypeStruct((m, n), out_dtype),
        compiler_params=_cparams(("parallel", "arbitrary"), need),
        name=name,
    )(x, w)


def _rms(xf, g):
    return xf * lax.rsqrt(jnp.mean(xf * xf, axis=-1, keepdims=True) + EPS) * g


def _prenorm_kernel(x_ref, g_ref, h_ref):
    h_ref[...] = _rms(x_ref[...], g_ref[...]).astype(h_ref.dtype)


def _prenorm(x, g, tm=256):
    m, d = x.shape
    return pl.pallas_call(
        _prenorm_kernel,
        grid=(m // tm,),
        in_specs=[pl.BlockSpec((tm, d), lambda i: (i, 0)), pl.BlockSpec((1, d), lambda i: (0, 0))],
        out_specs=pl.BlockSpec((tm, d), lambda i: (i, 0)),
        out_shape=jax.ShapeDtypeStruct((m, d), jnp.bfloat16),
        compiler_params=_cparams(("parallel",), 2 * tm * d * 6 + (8 << 20)),
        name="prenorm",
    )(x, g.reshape(1, d))


def _postnorm_kernel(x_ref, y_ref, gp_ref, gn_ref, xo_ref, h_ref):
    xn = x_ref[...] + _rms(y_ref[...], gp_ref[...])
    xo_ref[...] = xn
    h_ref[...] = _rms(xn, gn_ref[...]).astype(h_ref.dtype)


def _postnorm_last_kernel(x_ref, y_ref, gp_ref, xo_ref):
    xo_ref[...] = x_ref[...] + _rms(y_ref[...], gp_ref[...])


def _postnorm(x, y, g_post, g_next, tm=256):
    m, d = x.shape
    row = pl.BlockSpec((tm, d), lambda i: (i, 0))
    vec = pl.BlockSpec((1, d), lambda i: (0, 0))
    params = _cparams(("parallel",), 2 * tm * d * 14 + (8 << 20))
    if g_next is None:
        return pl.pallas_call(
            _postnorm_last_kernel, grid=(m // tm,), in_specs=[row, row, vec], out_specs=row,
            out_shape=jax.ShapeDtypeStruct((m, d), jnp.float32), compiler_params=params, name="postnorm_last",
        )(x, y, g_post.reshape(1, d)), None
    return pl.pallas_call(
        _postnorm_kernel, grid=(m // tm,), in_specs=[row, row, vec, vec], out_specs=[row, row],
        out_shape=[jax.ShapeDtypeStruct((m, d), jnp.float32), jax.ShapeDtypeStruct((m, d), jnp.bfloat16)],
        compiler_params=params, name="postnorm",
    )(x, y, g_post.reshape(1, d), g_next.reshape(1, d))


def _loggate_kernel(h_ref, w1_ref, w2_ref, b_ref, o_ref):
    glr = jnp.dot(h_ref[...], w1_ref[...], preferred_element_type=jnp.float32)
    z = jnp.dot(glr, w2_ref[...], preferred_element_type=jnp.float32,
                precision=lax.Precision.HIGHEST) + b_ref[...]
    o_ref[...] = (jnp.minimum(z, 0.0) - jnp.log1p(jnp.exp(-jnp.abs(z)))) * (1.0 / GLA_TAU)


def _loggate(h, w1, w2, b, tm=512):
    m, d = h.shape
    r = w1.shape[1]
    n = w2.shape[1]
    return pl.pallas_call(
        _loggate_kernel,
        grid=(m // tm,),
        in_specs=[pl.BlockSpec((tm, d), lambda i: (i, 0)), pl.BlockSpec((d, r), lambda i: (0, 0)),
                  pl.BlockSpec((r, n), lambda i: (0, 0)), pl.BlockSpec((1, n), lambda i: (0, 0))],
        out_specs=pl.BlockSpec((tm, n), lambda i: (i, 0)),
        out_shape=jax.ShapeDtypeStruct((m, n), jnp.float32),
        compiler_params=_cparams(("parallel",), 2 * (tm * d * 2 + tm * n * 4) + (16 << 20)),
        name="loggate",
    )(h, w1, w2, b.reshape(1, n))


def _gla_kernel(q_ref, k_ref, vm_ref, vt_ref, gm_ref, gt_ref, la_ref, nm_ref, nt_ref,
                om_ref, ot_ref, state_ref, b_ref, intra_ref, kf_ref, vf_ref):
    C = GLA_CHUNK
    DK = GLA_DK
    n_chunks = q_ref.shape[0] // C

    @pl.when(pl.program_id(2) == 0)
    def _():
        state_ref[...] = jnp.zeros_like(state_ref)

    row = lax.broadcasted_iota(jnp.int32, (C, C), 0)
    col = lax.broadcasted_iota(jnp.int32, (C, C), 1)
    causal = row >= col
    tril = causal.astype(jnp.bfloat16)
    eye = (row == col).astype(jnp.float32)
    lane2 = lax.broadcasted_iota(jnp.int32, (C, 2 * V7X_LANES), 1)
    lane1 = lax.broadcasted_iota(jnp.int32, (C, V7X_LANES), 1)
    row_id = lax.broadcasted_iota(jnp.int32, (C, 1), 0)
    scale = DK ** -0.5

    def chunk(c, carry):
        r0 = pl.multiple_of(c * C, C)
        rows = pl.ds(r0, C)
        la = la_ref[rows, :]
        la_hi = la.astype(jnp.bfloat16)
        la_lo = (la - la_hi.astype(jnp.float32)).astype(jnp.bfloat16)
        b_ref[...] = (jnp.dot(tril, la_hi, preferred_element_type=jnp.float32)
                      + jnp.dot(tril, la_lo, preferred_element_type=jnp.float32))
        vt = vt_ref[rows, :]
        gt = gt_ref[rows, :].astype(jnp.float32)
        y_tail = []
        for hh in range(2):
            sl = slice(hh * DK, (hh + 1) * DK)
            bh = b_ref[:, sl]
            b_last = b_ref[C - 1:C, sl]
            b_mid = b_ref[C // 2 - 1:C // 2, sl]
            q = q_ref[rows, sl].astype(jnp.float32) * scale
            k = k_ref[rows, sl].astype(jnp.float32)
            v = jnp.concatenate([vm_ref[rows, sl], vt], axis=1)

            tame = jnp.max(-b_last) <= GLA_SAFE_RANGE

            @pl.when(tame)
            def _():
                qr = (q * jnp.exp(bh - b_mid)).astype(jnp.bfloat16)
                kr = (k * jnp.exp(b_mid - bh)).astype(jnp.bfloat16)
                a = lax.dot_general(qr, kr, (((1,), (1,)), ((), ())), preferred_element_type=jnp.float32)
                a = jnp.where(causal, a, 0.0).astype(jnp.bfloat16)
                intra_ref[...] = jnp.dot(a, v, preferred_element_type=jnp.float32)

            @pl.when(jnp.logical_not(tame))
            def _():
                kf_ref[...] = k
                vf_ref[...] = v.astype(jnp.float32)

                def src(grp, acc):
                    s0 = pl.multiple_of(grp * 8, 8)
                    b8 = b_ref[pl.ds(s0, 8), sl]
                    k8 = kf_ref[pl.ds(s0, 8), :]
                    v8 = vf_ref[pl.ds(s0, 8), :]
                    for j in range(8):
                        p = q * jnp.exp(jnp.minimum(bh - b8[j:j + 1], 0.0)) * k8[j:j + 1]
                        a_s = jnp.sum(p, axis=-1, keepdims=True)
                        a_s = jnp.where(row_id >= s0 + j, a_s, 0.0)
                        acc = acc + a_s * v8[j:j + 1]
                    return acc
                intra_ref[...] = lax.fori_loop(0, C // 8, src, jnp.zeros((C, 2 * V7X_LANES), jnp.float32))

            st = state_ref[hh]
            qb = (q * jnp.exp(bh)).astype(jnp.bfloat16)
            o = intra_ref[...] + jnp.dot(qb, st.astype(jnp.bfloat16), preferred_element_type=jnp.float32)
            kd = (k * jnp.exp(b_last - bh)).astype(jnp.bfloat16)
            dec_col = jnp.sum(eye * jnp.exp(b_last), axis=-1, keepdims=True)
            state_ref[hh] = st * dec_col + lax.dot_general(
                kd, v, (((0,), (0,)), ((), ())), preferred_element_type=jnp.float32)

            valid = (lane2 < GLA_MAIN) | ((lane2 >= GLA_MAIN + hh * GLA_TAIL) & (lane2 < GLA_MAIN + (hh + 1) * GLA_TAIL))
            ss = jnp.sum(jnp.where(valid, o * o, 0.0), axis=-1, keepdims=True) * (1.0 / GLA_DV)
            gain = jnp.concatenate([nm_ref[:, sl], nt_ref[...]], axis=1)
            g = jnp.concatenate([gm_ref[rows, sl].astype(jnp.float32), gt], axis=1)
            y = (o * lax.rsqrt(ss + EPS) * gain) * (g * jax.nn.sigmoid(g))
            om_ref[rows, sl] = y[:, :GLA_MAIN].astype(om_ref.dtype)
            y_tail.append(y[:, GLA_MAIN:])
        ot_ref[rows, :] = jnp.where(lane1 < GLA_TAIL, y_tail[0], y_tail[1]).astype(ot_ref.dtype)
        return carry

    lax.fori_loop(0, n_chunks, chunk, 0)


def _gla(proj, la, nm, nt, batch, seq):
    m = proj.shape[0]
    R = GLA_ROWS
    nr = seq // R
    W2 = 2 * V7X_LANES

    def rowmap(off, width):
        base = off // width
        return lambda b, p, r: (b * nr + r, base + p)

    in_specs = [
        pl.BlockSpec((R, W2), rowmap(_P_GQ, W2)),
        pl.BlockSpec((R, W2), rowmap(_P_GK, W2)),
        pl.BlockSpec((R, W2), rowmap(_P_GVM, W2)),
        pl.BlockSpec((R, V7X_LANES), rowmap(_P_GVT, V7X_LANES)),
        pl.BlockSpec((R, W2), rowmap(_P_GGM, W2)),
        pl.BlockSpec((R, V7X_LANES), rowmap(_P_GGT, V7X_LANES)),
        pl.BlockSpec((R, W2), rowmap(0, W2)),
        pl.BlockSpec((1, W2), lambda b, p, r: (0, p)),
        pl.BlockSpec((1, V7X_LANES), lambda b, p, r: (0, p)),
    ]
    out_specs = [pl.BlockSpec((R, W2), rowmap(0, W2)), pl.BlockSpec((R, V7X_LANES), rowmap(0, V7X_LANES))]
    return pl.pallas_call(
        _gla_kernel,
        grid=(batch, GLA_HEADS // 2, nr),
        in_specs=in_specs,
        out_specs=out_specs,
        out_shape=[jax.ShapeDtypeStruct((m, GLA_MAIN_W), jnp.bfloat16),
                   jax.ShapeDtypeStruct((m, GLA_TAIL_W), jnp.bfloat16)],
        scratch_shapes=[pltpu.VMEM((2, GLA_DK, W2), jnp.float32),
                        pltpu.VMEM((GLA_CHUNK, W2), jnp.float32),
                        pltpu.VMEM((GLA_CHUNK, W2), jnp.float32),
                        pltpu.VMEM((GLA_CHUNK, GLA_DK), jnp.float32),
                        pltpu.VMEM((GLA_CHUNK, W2), jnp.float32)],
        compiler_params=_cparams(("parallel", "parallel", "arbitrary"), 32 << 20),
        name="gla",
    )(proj, proj, proj, proj, proj, proj, la, nm, nt)


def _gelu_tanh(x):
    return 0.5 * x * (1.0 + jnp.tanh(math.sqrt(2.0 / math.pi) * (x + 0.044715 * (x * x * x))))


def _sg_kernel(u_ref, v_ref, lg_ref, lb_ref, w_ref, bs_ref, o_ref):
    C = SG_CHUNK
    GW = SG_GROUP_WIDTH
    row = lax.broadcasted_iota(jnp.int32, (C, C), 0)
    col = lax.broadcasted_iota(jnp.int32, (C, C), 1)
    causal = row >= col
    for c in range(u_ref.shape[0] // C):
        rows = slice(c * C, (c + 1) * C)
        v = _gelu_tanh(v_ref[rows, :].astype(jnp.float32))
        mu = jnp.mean(v, axis=-1, keepdims=True)
        vc = v - mu
        var = jnp.mean(vc * vc, axis=-1, keepdims=True)
        vn = (vc * lax.rsqrt(var + EPS) * lg_ref[...] + lb_ref[...]).astype(jnp.bfloat16)
        for g in range(SG_GROUPS):
            cols = slice(g * GW, (g + 1) * GW)
            w = jnp.where(causal, w_ref[g], 0.0).astype(jnp.bfloat16)
            f = jnp.dot(w, vn[:, cols], preferred_element_type=jnp.float32) + bs_ref[g]
            u = _gelu_tanh(u_ref[rows, cols].astype(jnp.float32))
            o_ref[rows, cols] = (u * f).astype(o_ref.dtype)


def _spatial_gating(proj, ln_g, ln_b, w_s, b_s):
    m = proj.shape[0]
    R = SG_ROWS
    G, C = SG_GROUPS, SG_CHUNK
    return pl.pallas_call(
        _sg_kernel,
        grid=(m // R,),
        in_specs=[pl.BlockSpec((R, SG_W), lambda i: (i, _P_SU // SG_W)),
                  pl.BlockSpec((R, SG_W), lambda i: (i, _P_SV // SG_W)),
                  pl.BlockSpec((1, SG_W), lambda i: (0, 0)),
                  pl.BlockSpec((1, SG_W), lambda i: (0, 0)),
                  pl.BlockSpec((G, C, C), lambda i: (0, 0, 0)),
                  pl.BlockSpec((G, C, 1), lambda i: (0, 0, 0))],
        out_specs=pl.BlockSpec((R, SG_W), lambda i: (i, 0)),
        out_shape=jax.ShapeDtypeStruct((m, SG_W), jnp.bfloat16),
        compiler_params=_cparams(("parallel",), 32 << 20),
        name="spatial_gating",
    )(proj, proj, ln_g.reshape(1, SG_W), ln_b.reshape(1, SG_W), w_s, b_s.reshape(G, C, 1))


def _diff_attn_kernel(slopes_ref, q_ref, k_ref, v_ref, lq1_ref, lk1_ref, lq2_ref, lk2_ref, ng_ref,
                      o_ref, m_ref, l_ref, acc_ref, *, lam_init):
    TQ, TK, DK = ATTN_TQ, ATTN_TK, DIFF_DK
    h = pl.program_id(1)
    qi = pl.program_id(2)
    slope = slopes_ref[h]
    scale = DK ** -0.5

    m_ref[...] = jnp.full_like(m_ref, NEG_BIG)
    l_ref[...] = jnp.zeros_like(l_ref)
    acc_ref[...] = jnp.zeros_like(acc_ref)

    rel = (lax.broadcasted_iota(jnp.int32, (TQ, TK), 1) - lax.broadcasted_iota(jnp.int32, (TQ, TK), 0))
    rel_bias = rel.astype(jnp.float32) * slope

    def block(kb, masked):
        k0 = pl.multiple_of(kb * TK, TK)
        kblk = k_ref[pl.ds(k0, TK), :]
        vblk = v_ref[pl.ds(k0, TK), :]
        bias = rel_bias + slope * (kb * TK - qi * TQ).astype(jnp.float32)
        for mp in range(2):
            sl = slice(mp * DK, (mp + 1) * DK)
            s = lax.dot_general(q_ref[:, sl], kblk[:, sl], (((1,), (1,)), ((), ())),
                                preferred_element_type=jnp.float32) * scale + bias
            if masked:
                s = jnp.where(rel <= 0, s, NEG_BIG)
            m_old = m_ref[mp]
            m_new = jnp.maximum(m_old, jnp.max(s, axis=-1, keepdims=True))
            alpha = jnp.exp(m_old - m_new)
            p = jnp.exp(s - m_new)
            l_ref[mp] = alpha * l_ref[mp] + jnp.sum(p, axis=-1, keepdims=True)
            acc_ref[mp] = alpha * acc_ref[mp] + jnp.dot(p.astype(jnp.bfloat16), vblk,
                                                        preferred_element_type=jnp.float32)
            m_ref[mp] = m_new

    def full_block(kb, carry):
        block(kb, False)
        return carry

    lax.fori_loop(0, qi, full_block, 0)
    block(qi, True)

    lam = (jnp.exp(jnp.sum(lq1_ref[...] * lk1_ref[...])) - jnp.exp(jnp.sum(lq2_ref[...] * lk2_ref[...]))
           + lam_init)
    o = acc_ref[0] / l_ref[0] - lam * (acc_ref[1] / l_ref[1])
    y = _rms(o, ng_ref[...]) * (1.0 - lam_init)
    o_ref[...] = y.astype(o_ref.dtype)


def _diff_attention(proj, slopes, lq1, lk1, lq2, lk2, norm_g, lam_init, batch, seq):
    m = proj.shape[0]
    W = 2 * DIFF_DK
    nq = seq // ATTN_TQ
    vec = pl.BlockSpec((1, DIFF_DK), lambda b, h, i: (0, 0))
    kernel = functools.partial(_diff_attn_kernel, lam_init=lam_init)
    return pl.pallas_call(
        kernel,
        grid=(batch, DIFF_HEADS, nq),
        in_specs=[pl.BlockSpec(memory_space=pltpu.SMEM),
                  pl.BlockSpec((ATTN_TQ, W), lambda b, h, i: (b * nq + i, _P_DQ // W + h)),
                  pl.BlockSpec((seq, W), lambda b, h, i: (b, _P_DK // W + h)),
                  pl.BlockSpec((seq, DIFF_DV), lambda b, h, i: (b, _P_DV // DIFF_DV + h)),
                  vec, vec, vec, vec,
                  pl.BlockSpec((1, DIFF_DV), lambda b, h, i: (0, 0))],
        out_specs=pl.BlockSpec((ATTN_TQ, DIFF_DV), lambda b, h, i: (b * nq + i, h)),
        out_shape=jax.ShapeDtypeStruct((m, DIFF_V), jnp.bfloat16),
        scratch_shapes=[pltpu.VMEM((2, ATTN_TQ, 1), jnp.float32),
                        pltpu.VMEM((2, ATTN_TQ, 1), jnp.float32),
                        pltpu.VMEM((2, ATTN_TQ, DIFF_DV), jnp.float32)],
        compiler_params=_cparams(("parallel", "parallel", "arbitrary"), 32 << 20),
        name="diff_attention",
    )(slopes, proj, proj, proj, lq1.reshape(1, -1), lk1.reshape(1, -1), lq2.reshape(1, -1),
      lk2.reshape(1, -1), norm_g.reshape(1, -1))


def _merge_kernel(om_ref, ot_ref, osg_ref, od_ref, wm_ref, wt_ref, ws_ref, wd_ref,
                  g0_ref, g1_ref, g2_ref, o_ref):
    f32 = jnp.float32
    y_gla = (jnp.dot(om_ref[...], wm_ref[...], preferred_element_type=f32)
             + jnp.dot(ot_ref[...], wt_ref[...], preferred_element_type=f32))
    acc = jax.nn.sigmoid(g0_ref[...].astype(f32)) * y_gla
    acc += jax.nn.sigmoid(g1_ref[...].astype(f32)) * jnp.dot(osg_ref[...], ws_ref[...], preferred_element_type=f32)
    acc += jax.nn.sigmoid(g2_ref[...].astype(f32)) * jnp.dot(od_ref[...], wd_ref[...], preferred_element_type=f32)
    o_ref[...] = acc.astype(o_ref.dtype)


def _merge(om, ot, osg, od, wm, wt, ws, wd, proj, tm=512, tn=1024):
    m = om.shape[0]
    n = wm.shape[1]
    gpb = D_MODEL // tn

    def act(a):
        return pl.BlockSpec((tm, a.shape[1]), lambda i, j: (i, 0))

    def wgt(w):
        return pl.BlockSpec((w.shape[0], tn), lambda i, j: (0, j))

    def gate(idx):
        base = (_P_GATES + idx * D_MODEL) // tn
        return pl.BlockSpec((tm, tn), lambda i, j: (i, base + j))

    kin = om.shape[1] + ot.shape[1] + osg.shape[1] + od.shape[1]
    need = 2 * (tm * kin * 2 + kin * tn * 2 + 4 * tm * tn * 2) + 3 * tm * tn * 4 + (4 << 20)
    return pl.pallas_call(
        _merge_kernel,
        grid=(m // tm, n // tn),
        in_specs=[act(om), act(ot), act(osg), act(od), wgt(wm), wgt(wt), wgt(ws), wgt(wd),
                  gate(0), gate(1), gate(2)],
        out_specs=pl.BlockSpec((tm, tn), lambda i, j: (i, j)),
        out_shape=jax.ShapeDtypeStruct((m, n), jnp.bfloat16),
        compiler_params=_cparams(("parallel", "arbitrary"), need),
        name="merge",
    )(om, ot, osg, od, wm, wt, ws, wd, proj, proj, proj)


def _convgate_kernel(a_ref, u_ref, ah_ref, uh_ref, wa_ref, wu_ref, ba_ref, bu_ref, o_ref, *, blocks_per_seq):
    tr = a_ref.shape[0]
    first = (pl.program_id(0) % blocks_per_seq) == 0

    def conv(cur_ref, halo_ref, w_ref, b_ref):
        halo = jnp.where(first, 0.0, halo_ref[...].astype(jnp.float32))
        full = jnp.concatenate([halo, cur_ref[...].astype(jnp.float32)], axis=0)
        x1 = pltpu.roll(full, 1, axis=0)[CONV_HALO:]
        x2 = pltpu.roll(full, 2, axis=0)[CONV_HALO:]
        return b_ref[...] + w_ref[0:1, :] * x2 + w_ref[1:2, :] * x1 + w_ref[2:3, :] * full[CONV_HALO:]

    a = conv(a_ref, ah_ref, wa_ref, ba_ref)
    u = conv(u_ref, uh_ref, wu_ref, bu_ref)
    o_ref[...] = (_gelu_tanh(a) * u).astype(o_ref.dtype)


def _convgate(up, conv_w, conv_b, seq, tr=512, tc=512):
    m = up.shape[0]
    nc = D_FF // tc
    hb = tr // CONV_HALO

    def cur(off):
        return pl.BlockSpec((tr, tc), lambda i, j: (i, off + j))

    def halo(off):
        return pl.BlockSpec((CONV_HALO, tc), lambda i, j: (jnp.maximum(i * hb - 1, 0), off + j))

    def wspec(off):
        return pl.BlockSpec((CONV_WIDTH, tc), lambda i, j: (0, off + j))

    def bspec(off):
        return pl.BlockSpec((1, tc), lambda i, j: (0, off + j))

    kernel = functools.partial(_convgate_kernel, blocks_per_seq=seq // tr)
    return pl.pallas_call(
        kernel,
        grid=(m // tr, nc),
        in_specs=[cur(0), cur(nc), halo(0), halo(nc), wspec(0), wspec(nc), bspec(0), bspec(nc)],
        out_specs=pl.BlockSpec((tr, tc), lambda i, j: (i, j)),
        out_shape=jax.ShapeDtypeStruct((m, D_FF), jnp.bfloat16),
        compiler_params=_cparams(("parallel", "parallel"), 32 << 20),
        name="convgate",
    )(up, up, up, up, conv_w, conv_w, conv_b.reshape(1, -1), conv_b.reshape(1, -1))


def _split_heads_last(w):
    lead = w.shape[:-1]
    wh = w.reshape(lead + (GLA_HEADS, GLA_DV))
    return (wh[..., :GLA_MAIN].reshape(lead + (GLA_MAIN_W,)),
            wh[..., GLA_MAIN:].reshape(lead + (GLA_TAIL_W,)))


def _permute_w_in(w):
    gvm, gvt = _split_heads_last(w[:, _R_GV:_R_GG])
    ggm, ggt = _split_heads_last(w[:, _R_GG:_R_LR])
    parts = [w[:, _R_GATES:], w[:, _R_GQ:_R_GV], gvm, gvt, ggm, ggt, w[:, _R_SU:_R_GATES]]
    pad = jnp.zeros((w.shape[0], PROJ_COLS - _P_END), w.dtype)
    return jnp.concatenate(parts + [pad], axis=1).astype(jnp.bfloat16)


def _alibi_slopes(n):
    def pow2(mm):
        start = 2.0 ** (-8.0 / mm)
        return [start ** (i + 1) for i in range(mm)]
    if math.log2(n).is_integer():
        return pow2(n)
    p = 2 ** int(math.floor(math.log2(n)))
    return pow2(p) + pow2(2 * p)[0::2][: n - p]


def kernel(x, g_pre_mix, w_in, w_gla_lr, b_gla_lr, gla_norm_g, sg_ln_g, sg_ln_b, sg_w_s, sg_b_s,
           diff_lambda_q1, diff_lambda_k1, diff_lambda_q2, diff_lambda_k2, diff_norm_g,
           w_br_gla, w_br_sg, w_br_diff, w_o, g_post_mix, g_pre_ffn, w_up, conv_w, conv_b, w_down, g_post_ffn):
    B, S, D = x.shape
    M = B * S
    bf16 = jnp.bfloat16
    slopes = jnp.asarray(_alibi_slopes(DIFF_HEADS), jnp.float32)

    xf = x.reshape(M, D)
    h = _prenorm(xf, g_pre_mix[0])
    for l in range(DEPTH):
        w_proj = _permute_w_in(w_in[l])
        proj = _matmul(h, w_proj, 1024, PROJ_TN, bf16, "in_proj")

        w_lr_in = jnp.pad(w_in[l][:, _R_LR:_R_SU], ((0, 0), (0, V7X_LANES - GLA_RANK))).astype(bf16)
        w_lr = jnp.pad(w_gla_lr[l], ((0, V7X_LANES - GLA_RANK), (0, 0)))
        log_a = _loggate(h, w_lr_in, w_lr, b_gla_lr[l])

        nm = jnp.tile(gla_norm_g[l][:GLA_MAIN], GLA_HEADS).reshape(1, GLA_MAIN_W)
        nt = jnp.tile(gla_norm_g[l][GLA_MAIN:], GLA_HEADS).reshape(1, GLA_TAIL_W)
        o_gm, o_gt = _gla(proj, log_a, nm, nt, B, S)

        o_sg = _spatial_gating(proj, sg_ln_g[l], sg_ln_b[l], sg_w_s[l], sg_b_s[l])

        lam_init = 0.8 - 0.6 * math.exp(-0.3 * l)
        o_df = _diff_attention(proj, slopes, diff_lambda_q1[l], diff_lambda_k1[l], diff_lambda_q2[l],
                               diff_lambda_k2[l], diff_norm_g[l], lam_init, B, S)

        wgh = w_br_gla[l].reshape(GLA_HEADS, GLA_DV, D)
        wgm = wgh[:, :GLA_MAIN].reshape(GLA_MAIN_W, D)
        wgt = wgh[:, GLA_MAIN:].reshape(GLA_TAIL_W, D)
        merged = _merge(o_gm, o_gt, o_sg, o_df, wgm.astype(bf16), wgt.astype(bf16),
                        w_br_sg[l].astype(bf16), w_br_diff[l].astype(bf16), proj)
        y = _matmul(merged, w_o[l].astype(bf16), 1024, 1024, jnp.float32, "out_proj")
        xf, h = _postnorm(xf, y, g_post_mix[l], g_pre_ffn[l])

        up = _matmul(h, w_up[l].astype(bf16), 1024, 1024, bf16, "ffn_up")
        act = _convgate(up, conv_w[l], conv_b[l], S)
        f = _matmul(act, w_down[l].astype(bf16), 512, 1024, jnp.float32, "ffn_down")
        g_next = g_pre_mix[l + 1] if l + 1 < DEPTH else None
        xf, h = _postnorm(xf, f, g_post_ffn[l], g_next)
    return xf.reshape(B, S, D)
```

```python
import functools
import math

import jax
import jax.numpy as jnp
from jax import lax
from jax.experimental import pallas as pl
from jax.experimental.pallas import tpu as pltpu

D_MODEL = 4096
DEPTH = 4
GLA_HEADS, GLA_DK, GLA_DV, GLA_RANK, GLA_TAU = 8, 128, 192, 16, 16.0
SG_GROUPS, SG_GROUP_WIDTH, SG_CHUNK = 8, 128, 128
DIFF_HEADS, DIFF_DK, DIFF_DV = 6, 128, 256
D_FF = 6144
CONV_WIDTH = 3
N_BRANCH = 3
EPS = 1e-6

GLA_QK = GLA_HEADS * GLA_DK
GLA_V = GLA_HEADS * GLA_DV
SG_W = SG_GROUPS * SG_GROUP_WIDTH
DIFF_QK = DIFF_HEADS * 2 * DIFF_DK
DIFF_V = DIFF_HEADS * DIFF_DV
GATES_W = N_BRANCH * D_MODEL

_R_GQ = 0
_R_GK = _R_GQ + GLA_QK
_R_GV = _R_GK + GLA_QK
_R_GG = _R_GV + GLA_V
_R_LR = _R_GG + GLA_V
_R_SU = _R_LR + GLA_RANK
IN_COLS = _R_SU + 2 * SG_W + 2 * DIFF_QK + DIFF_V + GATES_W

_A_GQ, _A_GK, _A_GV, _A_GG = _R_GQ, _R_GK, _R_GV, _R_GG
PA_COLS = _R_LR
_B_SU = 0
_B_SV = _B_SU + SG_W
_B_DQ = _B_SV + SG_W
_B_DK = _B_DQ + DIFF_QK
_B_DV = _B_DK + DIFF_QK
_B_GATES = _B_DV + DIFF_V
PB_COLS = _B_GATES + GATES_W

V7X_LANES = 128
V7X_VMEM_BYTES = 64 * 1024 * 1024
VMEM_LIMIT_CAP = V7X_VMEM_BYTES - 8 * 1024 * 1024

MM_TN = 512
GLA_CHUNK = 128
GLA_ROWS = 512
GLA_SAFE_RANGE = 80.0
ATTN_TQ = 256
SG_ROWS = 512
CONV_HALO = 16
NEG_BIG = -1e30


def _vmem_limit(nbytes):
    return int(min(VMEM_LIMIT_CAP, max(32 * 1024 * 1024, nbytes)))


def _cparams(sem, nbytes):
    return pltpu.CompilerParams(dimension_semantics=sem, vmem_limit_bytes=_vmem_limit(nbytes))


CAST_ROWS = 512


def _cast_block(w_ref, wx_ref, wb_ref, shift):
    k = w_ref.shape[0]
    tn = wb_ref.shape[1]
    for r in range(0, k, CAST_ROWS):
        rows = slice(r, r + CAST_ROWS)
        if shift == 0:
            wb_ref[rows, :] = w_ref[rows, :].astype(wb_ref.dtype)
        else:
            w = jnp.concatenate([w_ref[rows, :], wx_ref[rows, :]], axis=1)
            w = pltpu.roll(w, tn + V7X_LANES - shift, axis=1)
            wb_ref[rows, :] = w[:, :tn].astype(wb_ref.dtype)


def _mm_kernel(x_ref, w_ref, o_ref, wb_ref):
    @pl.when(pl.program_id(1) == 0)
    def _():
        _cast_block(w_ref, None, wb_ref, 0)

    o_ref[...] = jnp.dot(x_ref[...], wb_ref[...], preferred_element_type=jnp.float32).astype(o_ref.dtype)


def _mm_shift_kernel(x_ref, w_ref, wx_ref, o_ref, wb_ref, *, shift):
    @pl.when(pl.program_id(1) == 0)
    def _():
        _cast_block(w_ref, wx_ref, wb_ref, shift)

    o_ref[...] = jnp.dot(x_ref[...], wb_ref[...], preferred_element_type=jnp.float32).astype(o_ref.dtype)


def _matmul(x, w, layer, col0, n, tm, out_dtype, name, tn=MM_TN):
    m, k = x.shape
    shift = col0 % tn
    base = col0 // tn
    osz = jnp.dtype(out_dtype).itemsize
    need = 2 * (tm * k * 2 + k * tn * 4 + tm * tn * osz) + k * tn * 2 + tm * tn * 4 + (4 << 20)
    x_spec = pl.BlockSpec((tm, k), lambda j, i: (i, 0))
    w_spec = pl.BlockSpec((None, k, tn), lambda j, i: (layer, 0, base + j))
    out_spec = pl.BlockSpec((tm, tn), lambda j, i: (i, j))
    common = dict(
        grid=(n // tn, m // tm),
        out_specs=out_spec,
        out_shape=jax.ShapeDtypeStruct((m, n), out_dtype),
        scratch_shapes=[pltpu.VMEM((k, tn), jnp.bfloat16)],
        name=name,
    )
    if shift == 0:
        return pl.pallas_call(_mm_kernel, in_specs=[x_spec, w_spec],
                              compiler_params=_cparams(("arbitrary", "arbitrary"), need), **common)(x, w)
    per = tn // V7X_LANES
    wx_spec = pl.BlockSpec((None, k, V7X_LANES), lambda j, i: (layer, 0, (base + j + 1) * per))
    need += 2 * k * V7X_LANES * 4 + CAST_ROWS * (tn + V7X_LANES) * 8
    return pl.pallas_call(functools.partial(_mm_shift_kernel, shift=shift), in_specs=[x_spec, w_spec, wx_spec],
                          compiler_params=_cparams(("arbitrary", "arbitrary"), need), **common)(x, w, w)


def _rms(xf, g):
    return xf * lax.rsqrt(jnp.mean(xf * xf, axis=-1, keepdims=True) + EPS) * g


def _prenorm_kernel(x_ref, g_ref, h_ref):
    h_ref[...] = _rms(x_ref[...], g_ref[...]).astype(h_ref.dtype)


def _prenorm(x, g, tm=256):
    m, d = x.shape
    return pl.pallas_call(
        _prenorm_kernel,
        grid=(m // tm,),
        in_specs=[pl.BlockSpec((tm, d), lambda i: (i, 0)), pl.BlockSpec((1, d), lambda i: (0, 0))],
        out_specs=pl.BlockSpec((tm, d), lambda i: (i, 0)),
        out_shape=jax.ShapeDtypeStruct((m, d), jnp.bfloat16),
        compiler_params=_cparams(("parallel",), 2 * tm * d * 6 + (8 << 20)),
        name="prenorm",
    )(x, g.reshape(1, d))


def _postnorm_kernel(x_ref, y_ref, gp_ref, gn_ref, xo_ref, h_ref):
    xn = x_ref[...] + _rms(y_ref[...].astype(jnp.float32), gp_ref[...])
    xo_ref[...] = xn
    h_ref[...] = _rms(xn, gn_ref[...]).astype(h_ref.dtype)


def _postnorm_last_kernel(x_ref, y_ref, gp_ref, xo_ref):
    xo_ref[...] = x_ref[...] + _rms(y_ref[...].astype(jnp.float32), gp_ref[...])


def _postnorm(x, y, g_post, g_next, tm=256):
    m, d = x.shape
    row = pl.BlockSpec((tm, d), lambda i: (i, 0))
    vec = pl.BlockSpec((1, d), lambda i: (0, 0))
    params = _cparams(("parallel",), 2 * tm * d * 12 + (8 << 20))
    if g_next is None:
        return pl.pallas_call(
            _postnorm_last_kernel, grid=(m // tm,), in_specs=[row, row, vec], out_specs=row,
            out_shape=jax.ShapeDtypeStruct((m, d), jnp.float32), compiler_params=params, name="postnorm_last",
        )(x, y, g_post.reshape(1, d)), None
    return pl.pallas_call(
        _postnorm_kernel, grid=(m // tm,), in_specs=[row, row, vec, vec], out_specs=[row, row],
        out_shape=[jax.ShapeDtypeStruct((m, d), jnp.float32), jax.ShapeDtypeStruct((m, d), jnp.bfloat16)],
        compiler_params=params, name="postnorm",
    )(x, y, g_post.reshape(1, d), g_next.reshape(1, d))


def _loggate_kernel(h_ref, w1_ref, w2_ref, b_ref, o_ref):
    glr = jnp.dot(h_ref[...], w1_ref[...], preferred_element_type=jnp.float32)
    z = jnp.dot(glr, w2_ref[...], preferred_element_type=jnp.float32,
                precision=lax.Precision.HIGHEST) + b_ref[...]
    o_ref[...] = (jnp.minimum(z, 0.0) - jnp.log1p(jnp.exp(-jnp.abs(z)))) * (1.0 / GLA_TAU)


def _loggate(h, w1, w2, b, tm=1024):
    m, d = h.shape
    r = w1.shape[1]
    n = w2.shape[1]
    return pl.pallas_call(
        _loggate_kernel,
        grid=(m // tm,),
        in_specs=[pl.BlockSpec((tm, d), lambda i: (i, 0)), pl.BlockSpec((d, r), lambda i: (0, 0)),
                  pl.BlockSpec((r, n), lambda i: (0, 0)), pl.BlockSpec((1, n), lambda i: (0, 0))],
        out_specs=pl.BlockSpec((tm, n), lambda i: (i, 0)),
        out_shape=jax.ShapeDtypeStruct((m, n), jnp.float32),
        compiler_params=_cparams(("parallel",), 2 * (tm * d * 2 + tm * n * 4) + (24 << 20)),
        name="loggate",
    )(h, w1, w2, b.reshape(1, n))


def _gla_kernel(q_ref, k_ref, v0_ref, v1_ref, v2_ref, g0_ref, g1_ref, g2_ref, la_ref, na_ref, nb_ref,
                o_ref, state_ref, b_ref, intra_ref, kf_ref, vf_ref):
    C = GLA_CHUNK
    DK = GLA_DK
    L = V7X_LANES
    n_chunks = q_ref.shape[0] // C
    half = GLA_DV - L

    @pl.when(pl.program_id(2) == 0)
    def _():
        state_ref[...] = jnp.zeros_like(state_ref)

    row = lax.broadcasted_iota(jnp.int32, (C, C), 0)
    col = lax.broadcasted_iota(jnp.int32, (C, C), 1)
    causal = row >= col
    tril = causal.astype(jnp.bfloat16)
    eye = (row == col).astype(jnp.float32)
    lane2 = lax.broadcasted_iota(jnp.int32, (C, 2 * L), 1)
    lane1 = lax.broadcasted_iota(jnp.int32, (C, L), 1)
    row_id = lax.broadcasted_iota(jnp.int32, (C, 1), 0)
    scale = DK ** -0.5
    own_v = (v0_ref, v2_ref)
    own_g = (g0_ref, g2_ref)
    gains = (na_ref, nb_ref)
    valid = (lane2 < L + half, (lane2 < L) | (lane2 >= L + half))

    for c in range(n_chunks):
        rows = slice(c * C, (c + 1) * C)
        la = la_ref[rows, :]
        la_hi = la.astype(jnp.bfloat16)
        la_lo = (la - la_hi.astype(jnp.float32)).astype(jnp.bfloat16)
        b_ref[rows, :] = (jnp.dot(tril, la_hi, preferred_element_type=jnp.float32)
                          + jnp.dot(tril, la_lo, preferred_element_type=jnp.float32))
    b_end = b_ref[C - 1:C, :]
    for c in range(1, n_chunks):
        b_end = jnp.minimum(b_end, b_ref[(c + 1) * C - 1:(c + 1) * C, :])
    tame = jnp.max(-b_end) <= GLA_SAFE_RANGE

    def load_qkv(rows, hh):
        sl = slice(hh * DK, (hh + 1) * DK)
        q = q_ref[rows, sl].astype(jnp.float32) * scale
        k = k_ref[rows, sl].astype(jnp.float32)
        v = jnp.concatenate([own_v[hh][rows, :], v1_ref[rows, :]], axis=1)
        return q, k, v

    @pl.when(tame)
    def _():
        for c in range(n_chunks):
            rows = slice(c * C, (c + 1) * C)
            mid = c * C + C // 2 - 1
            for hh in range(2):
                sl = slice(hh * DK, (hh + 1) * DK)
                q, k, v = load_qkv(rows, hh)
                bh = b_ref[rows, sl]
                b_mid = b_ref[mid:mid + 1, sl]
                qr = (q * jnp.exp(bh - b_mid)).astype(jnp.bfloat16)
                kr = (k * jnp.exp(b_mid - bh)).astype(jnp.bfloat16)
                a = lax.dot_general(qr, kr, (((1,), (1,)), ((), ())), preferred_element_type=jnp.float32)
                a = jnp.where(causal, a, 0.0).astype(jnp.bfloat16)
                intra_ref[hh, rows, :] = jnp.dot(a, v, preferred_element_type=jnp.float32)

    @pl.when(jnp.logical_not(tame))
    def _():
        def chunk(c, carry):
            r0 = pl.multiple_of(c * C, C)
            rows = pl.ds(r0, C)
            for hh in range(2):
                sl = slice(hh * DK, (hh + 1) * DK)
                q, k, v = load_qkv(rows, hh)
                bh = b_ref[rows, sl]
                kf_ref[...] = k
                vf_ref[...] = v.astype(jnp.float32)

                def src(grp, acc):
                    s0 = pl.multiple_of(grp * 8, 8)
                    b8 = b_ref[pl.ds(r0 + s0, 8), sl]
                    k8 = kf_ref[pl.ds(s0, 8), :]
                    v8 = vf_ref[pl.ds(s0, 8), :]
                    for j in range(8):
                        p = q * jnp.exp(jnp.minimum(bh - b8[j:j + 1], 0.0)) * k8[j:j + 1]
                        a_s = jnp.sum(p, axis=-1, keepdims=True)
                        a_s = jnp.where(row_id >= s0 + j, a_s, 0.0)
                        acc = acc + a_s * v8[j:j + 1]
                    return acc
                intra_ref[hh, rows, :] = lax.fori_loop(0, C // 8, src, jnp.zeros((C, 2 * L), jnp.float32))
            return carry
        lax.fori_loop(0, n_chunks, chunk, 0)

    for c in range(n_chunks):
        rows = slice(c * C, (c + 1) * C)
        last = (c + 1) * C - 1
        y_shared = []
        for hh in range(2):
            sl = slice(hh * DK, (hh + 1) * DK)
            q, k, v = load_qkv(rows, hh)
            bh = b_ref[rows, sl]
            b_last = b_ref[last:last + 1, sl]
            st = state_ref[hh]
            qb = (q * jnp.exp(bh)).astype(jnp.bfloat16)
            o = intra_ref[hh, rows, :] + jnp.dot(qb, st.astype(jnp.bfloat16), preferred_element_type=jnp.float32)
            kd = (k * jnp.exp(b_last - bh)).astype(jnp.bfloat16)
            dec_col = jnp.sum(eye * jnp.exp(b_last), axis=-1, keepdims=True)
            state_ref[hh] = st * dec_col + lax.dot_general(
                kd, v, (((0,), (0,)), ((), ())), preferred_element_type=jnp.float32)

            ss = jnp.sum(jnp.where(valid[hh], o * o, 0.0), axis=-1, keepdims=True) * (1.0 / GLA_DV)
            g = jnp.concatenate([own_g[hh][rows, :], g1_ref[rows, :]], axis=1).astype(jnp.float32)
            y = (o * lax.rsqrt(ss + EPS) * gains[hh][...]) * (g * jax.nn.sigmoid(g))
            o_ref[rows, 2 * L * hh:2 * L * hh + L] = y[:, :L].astype(o_ref.dtype)
            y_shared.append(y[:, L:])
        o_ref[rows, L:2 * L] = jnp.where(lane1 < half, y_shared[0], y_shared[1]).astype(o_ref.dtype)


def _gla(proj_a, la, norm_g, batch, seq):
    m = proj_a.shape[0]
    R = GLA_ROWS
    nr = seq // R
    L = V7X_LANES
    W2 = 2 * L
    half = GLA_DV - L
    pad = jnp.zeros((W2 - GLA_DV,), norm_g.dtype)
    gain_a = jnp.concatenate([norm_g, pad]).reshape(1, W2)
    gain_b = jnp.concatenate([norm_g[half:], pad, norm_g[:half]]).reshape(1, W2)

    def pair(off):
        base = off // W2
        return pl.BlockSpec((R, W2), lambda b, p, r: (b * nr + r, base + p))

    def tile(off, t):
        base = off // L + t
        return pl.BlockSpec((R, L), lambda b, p, r: (b * nr + r, base + 3 * p))

    const = pl.BlockSpec((1, W2), lambda b, p, r: (0, 0))
    in_specs = [pair(_A_GQ), pair(_A_GK),
                tile(_A_GV, 0), tile(_A_GV, 1), tile(_A_GV, 2),
                tile(_A_GG, 0), tile(_A_GG, 1), tile(_A_GG, 2),
                pl.BlockSpec((R, W2), lambda b, p, r: (b * nr + r, p)), const, const]
    return pl.pallas_call(
        _gla_kernel,
        grid=(batch, GLA_HEADS // 2, nr),
        in_specs=in_specs,
        out_specs=pl.BlockSpec((R, 3 * L), lambda b, p, r: (b * nr + r, p)),
        out_shape=jax.ShapeDtypeStruct((m, GLA_V), jnp.bfloat16),
        scratch_shapes=[pltpu.VMEM((2, GLA_DK, W2), jnp.float32),
                        pltpu.VMEM((R, W2), jnp.float32),
                        pltpu.VMEM((2, R, W2), jnp.float32),
                        pltpu.VMEM((GLA_CHUNK, GLA_DK), jnp.float32),
                        pltpu.VMEM((GLA_CHUNK, W2), jnp.float32)],
        compiler_params=_cparams(("parallel", "parallel", "arbitrary"), 32 << 20),
        name="gla",
    )(proj_a, proj_a, proj_a, proj_a, proj_a, proj_a, proj_a, proj_a, la, gain_a, gain_b)


def _gelu_tanh(x):
    return 0.5 * x * (1.0 + jnp.tanh(math.sqrt(2.0 / math.pi) * (x + 0.044715 * (x * x * x))))


def _sg_kernel(u_ref, v_ref, lg_ref, lb_ref, w_ref, bs_ref, o_ref):
    C = SG_CHUNK
    GW = SG_GROUP_WIDTH
    row = lax.broadcasted_iota(jnp.int32, (C, C), 0)
    col = lax.broadcasted_iota(jnp.int32, (C, C), 1)
    causal = row >= col
    for c in range(u_ref.shape[0] // C):
        rows = slice(c * C, (c + 1) * C)
        v = _gelu_tanh(v_ref[rows, :].astype(jnp.float32))
        mu = jnp.mean(v, axis=-1, keepdims=True)
        vc = v - mu
        var = jnp.mean(vc * vc, axis=-1, keepdims=True)
        vn = (vc * lax.rsqrt(var + EPS) * lg_ref[...] + lb_ref[...]).astype(jnp.bfloat16)
        for g in range(SG_GROUPS):
            cols = slice(g * GW, (g + 1) * GW)
            w = jnp.where(causal, w_ref[g], 0.0).astype(jnp.bfloat16)
            f = jnp.dot(w, vn[:, cols], preferred_element_type=jnp.float32) + bs_ref[g]
            u = _gelu_tanh(u_ref[rows, cols].astype(jnp.float32))
            o_ref[rows, cols] = (u * f).astype(o_ref.dtype)


def _spatial_gating(proj_b, ln_g, ln_b, w_s, b_s):
    m = proj_b.shape[0]
    R = SG_ROWS
    G, C = SG_GROUPS, SG_CHUNK
    return pl.pallas_call(
        _sg_kernel,
        grid=(m // R,),
        in_specs=[pl.BlockSpec((R, SG_W), lambda i: (i, _B_SU // SG_W)),
                  pl.BlockSpec((R, SG_W), lambda i: (i, _B_SV // SG_W)),
                  pl.BlockSpec((1, SG_W), lambda i: (0, 0)),
                  pl.BlockSpec((1, SG_W), lambda i: (0, 0)),
                  pl.BlockSpec((G, C, C), lambda i: (0, 0, 0)),
                  pl.BlockSpec((G, C, 1), lambda i: (0, 0, 0))],
        out_specs=pl.BlockSpec((R, SG_W), lambda i: (i, 0)),
        out_shape=jax.ShapeDtypeStruct((m, SG_W), jnp.bfloat16),
        compiler_params=_cparams(("parallel",), 32 << 20),
        name="spatial_gating",
    )(proj_b, proj_b, ln_g.reshape(1, SG_W), ln_b.reshape(1, SG_W), w_s, b_s.reshape(G, C, 1))


def _diff_attn_kernel(slopes_ref, q_ref, k_ref, v_ref, lq1_ref, lk1_ref, lq2_ref, lk2_ref, ng_ref,
                      o_ref, bias_ref, *, lam_init):
    TQ, DK = ATTN_TQ, DIFF_DK
    seq = q_ref.shape[0]
    slope = slopes_ref[pl.program_id(1)]
    scale = DK ** -0.5
    nt = (((1,), (1,)), ((), ()))

    col = lax.broadcasted_iota(jnp.int32, (TQ, seq), 1)
    row = lax.broadcasted_iota(jnp.int32, (TQ, seq), 0)
    bias_ref[...] = (col - row).astype(jnp.float32) * slope
    tri = lax.broadcasted_iota(jnp.int32, (TQ, TQ), 1) <= lax.broadcasted_iota(jnp.int32, (TQ, TQ), 0)

    lam = (jnp.exp(jnp.sum(lq1_ref[...] * lk1_ref[...])) - jnp.exp(jnp.sum(lq2_ref[...] * lk2_ref[...]))
           + lam_init)

    for qi in range(seq // TQ):
        q0 = qi * TQ
        blk = slice(q0, q0 + TQ)
        outs = []
        for mp in range(2):
            sl = slice(mp * DK, (mp + 1) * DK)
            q = q_ref[blk, sl]
            s_d = lax.dot_general(q, k_ref[blk, sl], nt, preferred_element_type=jnp.float32) * scale + bias_ref[:, blk]
            s_d = jnp.where(tri, s_d, NEG_BIG)
            m = jnp.max(s_d, axis=-1, keepdims=True)
            if qi > 0:
                s_p = (lax.dot_general(q, k_ref[0:q0, sl], nt, preferred_element_type=jnp.float32) * scale
                       + bias_ref[:, 0:q0])
                m = jnp.maximum(m, jnp.max(s_p, axis=-1, keepdims=True))
            p_d = jnp.exp(s_d - m)
            l = jnp.sum(p_d, axis=-1, keepdims=True)
            acc = jnp.dot(p_d.astype(jnp.bfloat16), v_ref[blk, :], preferred_element_type=jnp.float32)
            if qi > 0:
                p_p = jnp.exp(s_p - m)
                l = l + jnp.sum(p_p, axis=-1, keepdims=True)
                acc = acc + jnp.dot(p_p.astype(jnp.bfloat16), v_ref[0:q0, :], preferred_element_type=jnp.float32)
            outs.append(acc / l)
        o = outs[0] - lam * outs[1]
        o_ref[blk, :] = (_rms(o, ng_ref[...]) * (1.0 - lam_init)).astype(o_ref.dtype)


def _diff_attention(proj_b, slopes, lq1, lk1, lq2, lk2, norm_g, lam_init, batch, seq):
    m = proj_b.shape[0]
    W = 2 * DIFF_DK
    vec = pl.BlockSpec((1, DIFF_DK), lambda b, h: (0, 0))
    kernel = functools.partial(_diff_attn_kernel, lam_init=lam_init)
    return pl.pallas_call(
        kernel,
        grid=(batch, DIFF_HEADS),
        in_specs=[pl.BlockSpec(memory_space=pltpu.SMEM),
                  pl.BlockSpec((seq, W), lambda b, h: (b, _B_DQ // W + h)),
                  pl.BlockSpec((seq, W), lambda b, h: (b, _B_DK // W + h)),
                  pl.BlockSpec((seq, DIFF_DV), lambda b, h: (b, _B_DV // DIFF_DV + h)),
                  vec, vec, vec, vec,
                  pl.BlockSpec((1, DIFF_DV), lambda b, h: (0, 0))],
        out_specs=pl.BlockSpec((seq, DIFF_DV), lambda b, h: (b, h)),
        out_shape=jax.ShapeDtypeStruct((m, DIFF_V), jnp.bfloat16),
        scratch_shapes=[pltpu.VMEM((ATTN_TQ, seq), jnp.float32)],
        compiler_params=_cparams(("parallel", "arbitrary"), 48 << 20),
        name="diff_attention",
    )(slopes, proj_b, proj_b, proj_b, lq1.reshape(1, -1), lk1.reshape(1, -1), lq2.reshape(1, -1),
      lk2.reshape(1, -1), norm_g.reshape(1, -1))


def _merge_kernel(og_ref, os_ref, od_ref, wg_ref, ws_ref, wd_ref, g0_ref, g1_ref, g2_ref, o_ref,
                  wgb_ref, wsb_ref, wdb_ref):
    f32 = jnp.float32

    @pl.when(pl.program_id(1) == 0)
    def _():
        wgb_ref[...] = wg_ref[...].astype(wgb_ref.dtype)
        wsb_ref[...] = ws_ref[...].astype(wsb_ref.dtype)
        wdb_ref[...] = wd_ref[...].astype(wdb_ref.dtype)

    acc = jax.nn.sigmoid(g0_ref[...].astype(f32)) * jnp.dot(og_ref[...], wgb_ref[...], preferred_element_type=f32)
    acc += jax.nn.sigmoid(g1_ref[...].astype(f32)) * jnp.dot(os_ref[...], wsb_ref[...], preferred_element_type=f32)
    acc += jax.nn.sigmoid(g2_ref[...].astype(f32)) * jnp.dot(od_ref[...], wdb_ref[...], preferred_element_type=f32)
    o_ref[...] = acc.astype(o_ref.dtype)


def _merge(og, osg, od, wg, ws, wd, proj_b, layer, tm=1024, tn=MM_TN):
    m = og.shape[0]
    n = wg.shape[2]

    def act(a):
        return pl.BlockSpec((tm, a.shape[1]), lambda j, i: (i, 0))

    def wgt(w):
        return pl.BlockSpec((None, w.shape[1], tn), lambda j, i: (layer, 0, j))

    def gate(idx):
        base = (_B_GATES + idx * D_MODEL) // tn
        return pl.BlockSpec((tm, tn), lambda j, i: (i, base + j))

    kin = og.shape[1] + osg.shape[1] + od.shape[1]
    need = 2 * (tm * kin * 2 + kin * tn * 4 + 4 * tm * tn * 2) + kin * tn * 2 + 3 * tm * tn * 4 + (4 << 20)
    return pl.pallas_call(
        _merge_kernel,
        grid=(n // tn, m // tm),
        in_specs=[act(og), act(osg), act(od), wgt(wg), wgt(ws), wgt(wd), gate(0), gate(1), gate(2)],
        out_specs=pl.BlockSpec((tm, tn), lambda j, i: (i, j)),
        out_shape=jax.ShapeDtypeStruct((m, n), jnp.bfloat16),
        scratch_shapes=[pltpu.VMEM((w.shape[1], tn), jnp.bfloat16) for w in (wg, ws, wd)],
        compiler_params=_cparams(("arbitrary", "arbitrary"), need),
        name="merge",
    )(og, osg, od, wg, ws, wd, proj_b, proj_b, proj_b)


def _convgate_kernel(a_ref, u_ref, ah_ref, uh_ref, wa_ref, wu_ref, ba_ref, bu_ref, o_ref, *, blocks_per_seq):
    first = (pl.program_id(0) % blocks_per_seq) == 0

    def conv(cur_ref, halo_ref, w_ref, b_ref):
        halo = jnp.where(first, 0.0, halo_ref[...].astype(jnp.float32))
        full = jnp.concatenate([halo, cur_ref[...].astype(jnp.float32)], axis=0)
        x1 = pltpu.roll(full, 1, axis=0)[CONV_HALO:]
        x2 = pltpu.roll(full, 2, axis=0)[CONV_HALO:]
        return b_ref[...] + w_ref[0:1, :] * x2 + w_ref[1:2, :] * x1 + w_ref[2:3, :] * full[CONV_HALO:]

    a = conv(a_ref, ah_ref, wa_ref, ba_ref)
    u = conv(u_ref, uh_ref, wu_ref, bu_ref)
    o_ref[...] = (_gelu_tanh(a) * u).astype(o_ref.dtype)


def _convgate(up, conv_w, conv_b, seq, tr=512, tc=512):
    m = up.shape[0]
    nc = D_FF // tc
    hb = tr // CONV_HALO

    def cur(off):
        return pl.BlockSpec((tr, tc), lambda i, j: (i, off + j))

    def halo(off):
        return pl.BlockSpec((CONV_HALO, tc), lambda i, j: (jnp.maximum(i * hb - 1, 0), off + j))

    def wspec(off):
        return pl.BlockSpec((CONV_WIDTH, tc), lambda i, j: (0, off + j))

    def bspec(off):
        return pl.BlockSpec((1, tc), lambda i, j: (0, off + j))

    kernel = functools.partial(_convgate_kernel, blocks_per_seq=seq // tr)
    return pl.pallas_call(
        kernel,
        grid=(m // tr, nc),
        in_specs=[cur(0), cur(nc), halo(0), halo(nc), wspec(0), wspec(nc), bspec(0), bspec(nc)],
        out_specs=pl.BlockSpec((tr, tc), lambda i, j: (i, j)),
        out_shape=jax.ShapeDtypeStruct((m, D_FF), jnp.bfloat16),
        compiler_params=_cparams(("parallel", "parallel"), 32 << 20),
        name="convgate",
    )(up, up, up, up, conv_w, conv_w, conv_b.reshape(1, -1), conv_b.reshape(1, -1))


def _alibi_slopes(n):
    def pow2(mm):
        start = 2.0 ** (-8.0 / mm)
        return [start ** (i + 1) for i in range(mm)]
    if math.log2(n).is_integer():
        return pow2(n)
    p = 2 ** int(math.floor(math.log2(n)))
    return pow2(p) + pow2(2 * p)[0::2][: n - p]


def kernel(x, g_pre_mix, w_in, w_gla_lr, b_gla_lr, gla_norm_g, sg_ln_g, sg_ln_b, sg_w_s, sg_b_s,
           diff_lambda_q1, diff_lambda_k1, diff_lambda_q2, diff_lambda_k2, diff_norm_g,
           w_br_gla, w_br_sg, w_br_diff, w_o, g_post_mix, g_pre_ffn, w_up, conv_w, conv_b, w_down, g_post_ffn):
    B, S, D = x.shape
    M = B * S
    bf16 = jnp.bfloat16
    slopes = jnp.asarray(_alibi_slopes(DIFF_HEADS), jnp.float32)

    xf = x.reshape(M, D)
    h = _prenorm(xf, g_pre_mix[0])
    for l in range(DEPTH):
        proj_a = _matmul(h, w_in, l, _R_GQ, PA_COLS, 1024, bf16, "in_proj_a")
        proj_b = _matmul(h, w_in, l, _R_SU, PB_COLS, 1024, bf16, "in_proj_b")

        w_lr_in = jnp.pad(w_in[l, :, _R_LR:_R_SU], ((0, 0), (0, V7X_LANES - GLA_RANK))).astype(bf16)
        w_lr = jnp.pad(w_gla_lr[l], ((0, V7X_LANES - GLA_RANK), (0, 0)))
        log_a = _loggate(h, w_lr_in, w_lr, b_gla_lr[l])

        o_gla = _gla(proj_a, log_a, gla_norm_g[l], B, S)
        o_sg = _spatial_gating(proj_b, sg_ln_g[l], sg_ln_b[l], sg_w_s[l], sg_b_s[l])
        lam_init = 0.8 - 0.6 * math.exp(-0.3 * l)
        o_df = _diff_attention(proj_b, slopes, diff_lambda_q1[l], diff_lambda_k1[l], diff_lambda_q2[l],
                               diff_lambda_k2[l], diff_norm_g[l], lam_init, B, S)

        merged = _merge(o_gla, o_sg, o_df, w_br_gla, w_br_sg, w_br_diff, proj_b, l)
        y = _matmul(merged, w_o, l, 0, D, 1024, bf16, "out_proj")
        xf, h = _postnorm(xf, y, g_post_mix[l], g_pre_ffn[l])

        up = _matmul(h, w_up, l, 0, 2 * D_FF, 1024, bf16, "ffn_up")
        act = _convgate(up, conv_w[l], conv_b[l], S)
        f = _matmul(act, w_down, l, 0, D, 512, bf16, "ffn_down")
        g_next = g_pre_mix[l + 1] if l + 1 < DEPTH else None
        xf, h = _postnorm(xf, f, g_post_ffn[l], g_next)
    return xf.reshape(B, S, D)
```

```python
import functools
import math

import jax
import jax.numpy as jnp
from jax import lax
from jax.experimental import pallas as pl
from jax.experimental.pallas import tpu as pltpu

D_MODEL = 4096
DEPTH = 4
GLA_HEADS, GLA_DK, GLA_DV, GLA_RANK, GLA_TAU = 8, 128, 192, 16, 16.0
SG_GROUPS, SG_GROUP_WIDTH, SG_CHUNK = 8, 128, 128
DIFF_HEADS, DIFF_DK, DIFF_DV = 6, 128, 256
D_FF = 6144
CONV_WIDTH = 3
N_BRANCH = 3
EPS = 1e-6

GLA_QK = GLA_HEADS * GLA_DK
GLA_V = GLA_HEADS * GLA_DV
SG_W = SG_GROUPS * SG_GROUP_WIDTH
DIFF_QK = DIFF_HEADS * 2 * DIFF_DK
DIFF_V = DIFF_HEADS * DIFF_DV
GATES_W = N_BRANCH * D_MODEL

_R_GQ = 0
_R_GK = _R_GQ + GLA_QK
_R_GV = _R_GK + GLA_QK
_R_GG = _R_GV + GLA_V
_R_LR = _R_GG + GLA_V
_R_SU = _R_LR + GLA_RANK
IN_COLS = _R_SU + 2 * SG_W + 2 * DIFF_QK + DIFF_V + GATES_W

_A_GQ, _A_GK, _A_GV, _A_GG = _R_GQ, _R_GK, _R_GV, _R_GG
PA_COLS = _R_LR
_B_SU = 0
_B_SV = _B_SU + SG_W
_B_DQ = _B_SV + SG_W
_B_DK = _B_DQ + DIFF_QK
_B_DV = _B_DK + DIFF_QK
_B_GATES = _B_DV + DIFF_V
PB_COLS = _B_GATES + GATES_W

V7X_LANES = 128
V7X_VMEM_BYTES = 64 * 1024 * 1024
VMEM_LIMIT_CAP = V7X_VMEM_BYTES - 8 * 1024 * 1024

MM_TN = 512
GLA_CHUNK = 128
GLA_ROWS = 512
GLA_SAFE_RANGE = 80.0
ATTN_TQ = 256
SG_ROWS = 512
CONV_HALO = 8
NEG_BIG = -1e30


def _vmem_limit(nbytes):
    return int(min(VMEM_LIMIT_CAP, max(32 * 1024 * 1024, nbytes)))


def _cparams(sem, nbytes):
    return pltpu.CompilerParams(dimension_semantics=sem, vmem_limit_bytes=_vmem_limit(nbytes))


CAST_CHUNK = 1024
BF16_ROWS = 16

_NN = (((1,), (0,)), ((), ()))
_NT = (((1,), (1,)), ((), ()))


def _mm_kernel(x_ref, w_ref, o_ref, wb_ref):
    @pl.when(pl.program_id(1) == 0)
    def _():
        for r in range(0, w_ref.shape[0], CAST_CHUNK):
            wb_ref[r:r + CAST_CHUNK, :] = w_ref[r:r + CAST_CHUNK, :].astype(wb_ref.dtype)

    o_ref[...] = lax.dot_general(x_ref[...], wb_ref[...], _NN, preferred_element_type=jnp.float32).astype(o_ref.dtype)


def _mm_t_kernel(x_ref, w_ref, *rest, shift):
    wx_ref = rest[0] if shift else None
    o_ref, wb_ref = rest[-2:]
    tn, k = wb_ref.shape

    @pl.when(pl.program_id(1) == 0)
    def _():
        for c in range(0, k, CAST_CHUNK):
            cols = slice(c, c + CAST_CHUNK)
            wb_ref[0:tn - shift, cols] = w_ref[shift:tn, cols].astype(wb_ref.dtype)
            if shift:
                wb_ref[tn - shift:tn, cols] = wx_ref[:, cols].astype(wb_ref.dtype)

    o_ref[...] = lax.dot_general(x_ref[...], wb_ref[...], _NT, preferred_element_type=jnp.float32).astype(o_ref.dtype)


def _matmul(x, w, layer, col0, n, tm, out_dtype, name, tn=MM_TN, transposed=False):
    m, k = x.shape
    shift = col0 % tn
    base = col0 // tn
    osz = jnp.dtype(out_dtype).itemsize
    need = 2 * (tm * k * 2 + k * tn * 4 + tm * tn * osz) + k * tn * 2 + tm * tn * 4 + (6 << 20)
    x_spec = pl.BlockSpec((tm, k), lambda j, i: (i, 0))
    common = dict(
        grid=(n // tn, m // tm),
        out_specs=pl.BlockSpec((tm, tn), lambda j, i: (i, j)),
        out_shape=jax.ShapeDtypeStruct((m, n), out_dtype),
        compiler_params=_cparams(("arbitrary", "arbitrary"), need),
        name=name,
    )
    if not transposed:
        assert shift == 0
        w_spec = pl.BlockSpec((None, k, tn), lambda j, i: (layer, 0, base + j))
        return pl.pallas_call(_mm_kernel, in_specs=[x_spec, w_spec],
                              scratch_shapes=[pltpu.VMEM((k, tn), jnp.bfloat16)], **common)(x, w)
    assert shift % BF16_ROWS == 0
    w_spec = pl.BlockSpec((None, tn, k), lambda j, i: (layer, base + j, 0))
    in_specs, args = [x_spec, w_spec], [x, w]
    if shift:
        per = tn // shift
        in_specs.append(pl.BlockSpec((None, shift, k), lambda j, i: (layer, (base + j + 1) * per, 0)))
        args.append(w)
    return pl.pallas_call(functools.partial(_mm_t_kernel, shift=shift), in_specs=in_specs,
                          scratch_shapes=[pltpu.VMEM((tn, k), jnp.bfloat16)], **common)(*args)


def _rms(xf, g):
    return xf * lax.rsqrt(jnp.mean(xf * xf, axis=-1, keepdims=True) + EPS) * g


def _prenorm_kernel(x_ref, g_ref, h_ref):
    h_ref[...] = _rms(x_ref[...], g_ref[...]).astype(h_ref.dtype)


def _prenorm(x, g, tm=256):
    m, d = x.shape
    return pl.pallas_call(
        _prenorm_kernel,
        grid=(m // tm,),
        in_specs=[pl.BlockSpec((tm, d), lambda i: (i, 0)), pl.BlockSpec((1, d), lambda i: (0, 0))],
        out_specs=pl.BlockSpec((tm, d), lambda i: (i, 0)),
        out_shape=jax.ShapeDtypeStruct((m, d), jnp.bfloat16),
        compiler_params=_cparams(("parallel",), 2 * tm * d * 6 + (8 << 20)),
        name="prenorm",
    )(x, g.reshape(1, d))


def _postnorm_kernel(x_ref, y_ref, gp_ref, gn_ref, xo_ref, h_ref):
    xn = x_ref[...] + _rms(y_ref[...].astype(jnp.float32), gp_ref[...])
    xo_ref[...] = xn
    h_ref[...] = _rms(xn, gn_ref[...]).astype(h_ref.dtype)


def _postnorm_last_kernel(x_ref, y_ref, gp_ref, xo_ref):
    xo_ref[...] = x_ref[...] + _rms(y_ref[...].astype(jnp.float32), gp_ref[...])


def _postnorm(x, y, g_post, g_next, tm=256):
    m, d = x.shape
    row = pl.BlockSpec((tm, d), lambda i: (i, 0))
    vec = pl.BlockSpec((1, d), lambda i: (0, 0))
    params = _cparams(("parallel",), 2 * tm * d * 12 + (8 << 20))
    if g_next is None:
        return pl.pallas_call(
            _postnorm_last_kernel, grid=(m // tm,), in_specs=[row, row, vec], out_specs=row,
            out_shape=jax.ShapeDtypeStruct((m, d), jnp.float32), compiler_params=params, name="postnorm_last",
        )(x, y, g_post.reshape(1, d)), None
    return pl.pallas_call(
        _postnorm_kernel, grid=(m // tm,), in_specs=[row, row, vec, vec], out_specs=[row, row],
        out_shape=[jax.ShapeDtypeStruct((m, d), jnp.float32), jax.ShapeDtypeStruct((m, d), jnp.bfloat16)],
        compiler_params=params, name="postnorm",
    )(x, y, g_post.reshape(1, d), g_next.reshape(1, d))


def _loggate_kernel(h_ref, w1_ref, w2_ref, b_ref, o_ref, w1b_ref):
    @pl.when(pl.program_id(0) == 0)
    def _():
        w1b_ref[...] = w1_ref[...].astype(w1b_ref.dtype)

    glr = lax.dot_general(h_ref[...], w1b_ref[...], _NT, preferred_element_type=jnp.float32)
    z = jnp.dot(glr, w2_ref[...], preferred_element_type=jnp.float32,
                precision=lax.Precision.HIGHEST) + b_ref[...]
    o_ref[...] = (jnp.minimum(z, 0.0) - jnp.log1p(jnp.exp(-jnp.abs(z)))) * (1.0 / GLA_TAU)


def _loggate(h, w_in_t, layer, w2, b, tm=1024):
    m, d = h.shape
    r, n = w2.shape
    return pl.pallas_call(
        _loggate_kernel,
        grid=(m // tm,),
        in_specs=[pl.BlockSpec((tm, d), lambda i: (i, 0)),
                  pl.BlockSpec((None, r, d), lambda i: (layer, _R_LR // r, 0)),
                  pl.BlockSpec((r, n), lambda i: (0, 0)), pl.BlockSpec((1, n), lambda i: (0, 0))],
        out_specs=pl.BlockSpec((tm, n), lambda i: (i, 0)),
        out_shape=jax.ShapeDtypeStruct((m, n), jnp.float32),
        scratch_shapes=[pltpu.VMEM((r, d), jnp.bfloat16)],
        compiler_params=_cparams(("arbitrary",), 2 * (tm * d * 2 + tm * n * 4) + (24 << 20)),
        name="loggate",
    )(h, w_in_t, w2, b.reshape(1, n))


def _gla_kernel(q_ref, k_ref, v0_ref, v1_ref, v2_ref, g0_ref, g1_ref, g2_ref, la_ref, na_ref, nb_ref,
                o_ref, state_ref, b_ref, intra_ref, kf_ref, vf_ref):
    C = GLA_CHUNK
    DK = GLA_DK
    L = V7X_LANES
    n_chunks = q_ref.shape[0] // C
    half = GLA_DV - L

    @pl.when(pl.program_id(2) == 0)
    def _():
        state_ref[...] = jnp.zeros_like(state_ref)

    row = lax.broadcasted_iota(jnp.int32, (C, C), 0)
    col = lax.broadcasted_iota(jnp.int32, (C, C), 1)
    causal = row >= col
    tril = causal.astype(jnp.bfloat16)
    eye = (row == col).astype(jnp.float32)
    lane2 = lax.broadcasted_iota(jnp.int32, (C, 2 * L), 1)
    lane1 = lax.broadcasted_iota(jnp.int32, (C, L), 1)
    row_id = lax.broadcasted_iota(jnp.int32, (C, 1), 0)
    scale = DK ** -0.5
    own_v = (v0_ref, v2_ref)
    own_g = (g0_ref, g2_ref)
    gains = (na_ref, nb_ref)
    valid = (lane2 < L + half, (lane2 < L) | (lane2 >= L + half))

    for c in range(n_chunks):
        rows = slice(c * C, (c + 1) * C)
        la = la_ref[rows, :]
        la_hi = la.astype(jnp.bfloat16)
        la_lo = (la - la_hi.astype(jnp.float32)).astype(jnp.bfloat16)
        b_ref[rows, :] = (jnp.dot(tril, la_hi, preferred_element_type=jnp.float32)
                          + jnp.dot(tril, la_lo, preferred_element_type=jnp.float32))
    b_end = b_ref[C - 1:C, :]
    for c in range(1, n_chunks):
        b_end = jnp.minimum(b_end, b_ref[(c + 1) * C - 1:(c + 1) * C, :])
    tame = jnp.max(-b_end) <= GLA_SAFE_RANGE

    def load_qkv(rows, hh):
        sl = slice(hh * DK, (hh + 1) * DK)
        q = q_ref[rows, sl].astype(jnp.float32) * scale
        k = k_ref[rows, sl].astype(jnp.float32)
        v = jnp.concatenate([own_v[hh][rows, :], v1_ref[rows, :]], axis=1)
        return q, k, v

    @pl.when(tame)
    def _():
        for c in range(n_chunks):
            rows = slice(c * C, (c + 1) * C)
            mid = c * C + C // 2 - 1
            for hh in range(2):
                sl = slice(hh * DK, (hh + 1) * DK)
                q, k, v = load_qkv(rows, hh)
                bh = b_ref[rows, sl]
                b_mid = b_ref[mid:mid + 1, sl]
                qr = (q * jnp.exp(bh - b_mid)).astype(jnp.bfloat16)
                kr = (k * jnp.exp(b_mid - bh)).astype(jnp.bfloat16)
                a = lax.dot_general(qr, kr, (((1,), (1,)), ((), ())), preferred_element_type=jnp.float32)
                a = jnp.where(causal, a, 0.0).astype(jnp.bfloat16)
                intra_ref[hh, rows, :] = jnp.dot(a, v, preferred_element_type=jnp.float32)

    @pl.when(jnp.logical_not(tame))
    def _():
        def chunk(c, carry):
            r0 = pl.multiple_of(c * C, C)
            rows = pl.ds(r0, C)
            for hh in range(2):
                sl = slice(hh * DK, (hh + 1) * DK)
                q, k, v = load_qkv(rows, hh)
                bh = b_ref[rows, sl]
                kf_ref[...] = k
                vf_ref[...] = v.astype(jnp.float32)

                def src(grp, acc):
                    s0 = pl.multiple_of(grp * 8, 8)
                    b8 = b_ref[pl.ds(r0 + s0, 8), sl]
                    k8 = kf_ref[pl.ds(s0, 8), :]
                    v8 = vf_ref[pl.ds(s0, 8), :]
                    for j in range(8):
                        p = q * jnp.exp(jnp.minimum(bh - b8[j:j + 1], 0.0)) * k8[j:j + 1]
                        a_s = jnp.sum(p, axis=-1, keepdims=True)
                        a_s = jnp.where(row_id >= s0 + j, a_s, 0.0)
                        acc = acc + a_s * v8[j:j + 1]
                    return acc
                intra_ref[hh, rows, :] = lax.fori_loop(0, C // 8, src, jnp.zeros((C, 2 * L), jnp.float32))
            return carry
        lax.fori_loop(0, n_chunks, chunk, 0)

    for c in range(n_chunks):
        rows = slice(c * C, (c + 1) * C)
        last = (c + 1) * C - 1
        y_shared = []
        for hh in range(2):
            sl = slice(hh * DK, (hh + 1) * DK)
            q, k, v = load_qkv(rows, hh)
            bh = b_ref[rows, sl]
            b_last = b_ref[last:last + 1, sl]
            st = state_ref[hh]
            qb = (q * jnp.exp(bh)).astype(jnp.bfloat16)
            o = intra_ref[hh, rows, :] + jnp.dot(qb, st.astype(jnp.bfloat16), preferred_element_type=jnp.float32)
            kd = (k * jnp.exp(b_last - bh)).astype(jnp.bfloat16)
            dec_col = jnp.sum(eye * jnp.exp(b_last), axis=-1, keepdims=True)
            state_ref[hh] = st * dec_col + lax.dot_general(
                kd, v, (((0,), (0,)), ((), ())), preferred_element_type=jnp.float32)

            ss = jnp.sum(jnp.where(valid[hh], o * o, 0.0), axis=-1, keepdims=True) * (1.0 / GLA_DV)
            g = jnp.concatenate([own_g[hh][rows, :], g1_ref[rows, :]], axis=1).astype(jnp.float32)
            y = (o * lax.rsqrt(ss + EPS) * gains[hh][...]) * (g * jax.nn.sigmoid(g))
            o_ref[rows, 2 * L * hh:2 * L * hh + L] = y[:, :L].astype(o_ref.dtype)
            y_shared.append(y[:, L:])
        o_ref[rows, L:2 * L] = jnp.where(lane1 < half, y_shared[0], y_shared[1]).astype(o_ref.dtype)


def _gla(proj_a, la, norm_g, batch, seq):
    m = proj_a.shape[0]
    R = GLA_ROWS
    nr = seq // R
    L = V7X_LANES
    W2 = 2 * L
    half = GLA_DV - L
    pad = jnp.zeros((W2 - GLA_DV,), norm_g.dtype)
    gain_a = jnp.concatenate([norm_g, pad]).reshape(1, W2)
    gain_b = jnp.concatenate([norm_g[half:], pad, norm_g[:half]]).reshape(1, W2)

    def pair(off):
        base = off // W2
        return pl.BlockSpec((R, W2), lambda b, p, r: (b * nr + r, base + p))

    def tile(off, t):
        base = off // L + t
        return pl.BlockSpec((R, L), lambda b, p, r: (b * nr + r, base + 3 * p))

    const = pl.BlockSpec((1, W2), lambda b, p, r: (0, 0))
    in_specs = [pair(_A_GQ), pair(_A_GK),
                tile(_A_GV, 0), tile(_A_GV, 1), tile(_A_GV, 2),
                tile(_A_GG, 0), tile(_A_GG, 1), tile(_A_GG, 2),
                pl.BlockSpec((R, W2), lambda b, p, r: (b * nr + r, p)), const, const]
    return pl.pallas_call(
        _gla_kernel,
        grid=(batch, GLA_HEADS // 2, nr),
        in_specs=in_specs,
        out_specs=pl.BlockSpec((R, 3 * L), lambda b, p, r: (b * nr + r, p)),
        out_shape=jax.ShapeDtypeStruct((m, GLA_V), jnp.bfloat16),
        scratch_shapes=[pltpu.VMEM((2, GLA_DK, W2), jnp.float32),
                        pltpu.VMEM((R, W2), jnp.float32),
                        pltpu.VMEM((2, R, W2), jnp.float32),
                        pltpu.VMEM((GLA_CHUNK, GLA_DK), jnp.float32),
                        pltpu.VMEM((GLA_CHUNK, W2), jnp.float32)],
        compiler_params=_cparams(("parallel", "parallel", "arbitrary"), 32 << 20),
        name="gla",
    )(proj_a, proj_a, proj_a, proj_a, proj_a, proj_a, proj_a, proj_a, la, gain_a, gain_b)


def _gelu_tanh(x):
    return 0.5 * x * (1.0 + jnp.tanh(math.sqrt(2.0 / math.pi) * (x + 0.044715 * (x * x * x))))


def _sg_kernel(u_ref, v_ref, lg_ref, lb_ref, w_ref, bs_ref, o_ref):
    C = SG_CHUNK
    GW = SG_GROUP_WIDTH
    row = lax.broadcasted_iota(jnp.int32, (C, C), 0)
    col = lax.broadcasted_iota(jnp.int32, (C, C), 1)
    causal = row >= col
    for c in range(u_ref.shape[0] // C):
        rows = slice(c * C, (c + 1) * C)
        v = _gelu_tanh(v_ref[rows, :].astype(jnp.float32))
        mu = jnp.mean(v, axis=-1, keepdims=True)
        vc = v - mu
        var = jnp.mean(vc * vc, axis=-1, keepdims=True)
        vn = (vc * lax.rsqrt(var + EPS) * lg_ref[...] + lb_ref[...]).astype(jnp.bfloat16)
        for g in range(SG_GROUPS):
            cols = slice(g * GW, (g + 1) * GW)
            w = jnp.where(causal, w_ref[g], 0.0).astype(jnp.bfloat16)
            f = jnp.dot(w, vn[:, cols], preferred_element_type=jnp.float32) + bs_ref[g]
            u = _gelu_tanh(u_ref[rows, cols].astype(jnp.float32))
            o_ref[rows, cols] = (u * f).astype(o_ref.dtype)


def _spatial_gating(proj_b, ln_g, ln_b, w_s, b_s):
    m = proj_b.shape[0]
    R = SG_ROWS
    G, C = SG_GROUPS, SG_CHUNK
    return pl.pallas_call(
        _sg_kernel,
        grid=(m // R,),
        in_specs=[pl.BlockSpec((R, SG_W), lambda i: (i, _B_SU // SG_W)),
                  pl.BlockSpec((R, SG_W), lambda i: (i, _B_SV // SG_W)),
                  pl.BlockSpec((1, SG_W), lambda i: (0, 0)),
                  pl.BlockSpec((1, SG_W), lambda i: (0, 0)),
                  pl.BlockSpec((G, C, C), lambda i: (0, 0, 0)),
                  pl.BlockSpec((G, C, 1), lambda i: (0, 0, 0))],
        out_specs=pl.BlockSpec((R, SG_W), lambda i: (i, 0)),
        out_shape=jax.ShapeDtypeStruct((m, SG_W), jnp.bfloat16),
        compiler_params=_cparams(("parallel",), 32 << 20),
        name="spatial_gating",
    )(proj_b, proj_b, ln_g.reshape(1, SG_W), ln_b.reshape(1, SG_W), w_s, b_s.reshape(G, C, 1))


def _diff_attn_kernel(slopes_ref, q_ref, k_ref, v_ref, lq1_ref, lk1_ref, lq2_ref, lk2_ref, ng_ref,
                      o_ref, bias_ref, *, lam_init):
    TQ, DK = ATTN_TQ, DIFF_DK
    seq = q_ref.shape[0]
    slope = slopes_ref[pl.program_id(1)]
    scale = DK ** -0.5
    nt = (((1,), (1,)), ((), ()))

    col = lax.broadcasted_iota(jnp.int32, (TQ, seq), 1)
    row = lax.broadcasted_iota(jnp.int32, (TQ, seq), 0)
    bias_ref[...] = (col - row).astype(jnp.float32) * slope
    tri = lax.broadcasted_iota(jnp.int32, (TQ, TQ), 1) <= lax.broadcasted_iota(jnp.int32, (TQ, TQ), 0)

    lam = (jnp.exp(jnp.sum(lq1_ref[...] * lk1_ref[...])) - jnp.exp(jnp.sum(lq2_ref[...] * lk2_ref[...]))
           + lam_init)

    for qi in range(seq // TQ):
        q0 = qi * TQ
        blk = slice(q0, q0 + TQ)
        outs = []
        for mp in range(2):
            sl = slice(mp * DK, (mp + 1) * DK)
            q = q_ref[blk, sl]
            s_d = lax.dot_general(q, k_ref[blk, sl], nt, preferred_element_type=jnp.float32) * scale + bias_ref[:, blk]
            s_d = jnp.where(tri, s_d, NEG_BIG)
            m = jnp.max(s_d, axis=-1, keepdims=True)
            if qi > 0:
                s_p = (lax.dot_general(q, k_ref[0:q0, sl], nt, preferred_element_type=jnp.float32) * scale
                       + bias_ref[:, 0:q0])
                m = jnp.maximum(m, jnp.max(s_p, axis=-1, keepdims=True))
            p_d = jnp.exp(s_d - m)
            l = jnp.sum(p_d, axis=-1, keepdims=True)
            acc = jnp.dot(p_d.astype(jnp.bfloat16), v_ref[blk, :], preferred_element_type=jnp.float32)
            if qi > 0:
                p_p = jnp.exp(s_p - m)
                l = l + jnp.sum(p_p, axis=-1, keepdims=True)
                acc = acc + jnp.dot(p_p.astype(jnp.bfloat16), v_ref[0:q0, :], preferred_element_type=jnp.float32)
            outs.append(acc / l)
        o = outs[0] - lam * outs[1]
        o_ref[blk, :] = (_rms(o, ng_ref[...]) * (1.0 - lam_init)).astype(o_ref.dtype)


def _diff_attention(proj_b, slopes, lq1, lk1, lq2, lk2, norm_g, lam_init, batch, seq):
    m = proj_b.shape[0]
    W = 2 * DIFF_DK
    vec = pl.BlockSpec((1, DIFF_DK), lambda b, h: (0, 0))
    kernel = functools.partial(_diff_attn_kernel, lam_init=lam_init)
    return pl.pallas_call(
        kernel,
        grid=(batch, DIFF_HEADS),
        in_specs=[pl.BlockSpec(memory_space=pltpu.SMEM),
                  pl.BlockSpec((seq, W), lambda b, h: (b, _B_DQ // W + h)),
                  pl.BlockSpec((seq, W), lambda b, h: (b, _B_DK // W + h)),
                  pl.BlockSpec((seq, DIFF_DV), lambda b, h: (b, _B_DV // DIFF_DV + h)),
                  vec, vec, vec, vec,
                  pl.BlockSpec((1, DIFF_DV), lambda b, h: (0, 0))],
        out_specs=pl.BlockSpec((seq, DIFF_DV), lambda b, h: (b, h)),
        out_shape=jax.ShapeDtypeStruct((m, DIFF_V), jnp.bfloat16),
        scratch_shapes=[pltpu.VMEM((ATTN_TQ, seq), jnp.float32)],
        compiler_params=_cparams(("parallel", "arbitrary"), 48 << 20),
        name="diff_attention",
    )(slopes, proj_b, proj_b, proj_b, lq1.reshape(1, -1), lk1.reshape(1, -1), lq2.reshape(1, -1),
      lk2.reshape(1, -1), norm_g.reshape(1, -1))


def _merge_kernel(og_ref, os_ref, od_ref, wg_ref, ws_ref, wd_ref, g0_ref, g1_ref, g2_ref, o_ref,
                  wgb_ref, wsb_ref, wdb_ref):
    f32 = jnp.float32

    @pl.when(pl.program_id(1) == 0)
    def _():
        wgb_ref[...] = wg_ref[...].astype(wgb_ref.dtype)
        wsb_ref[...] = ws_ref[...].astype(wsb_ref.dtype)
        wdb_ref[...] = wd_ref[...].astype(wdb_ref.dtype)

    acc = jax.nn.sigmoid(g0_ref[...].astype(f32)) * jnp.dot(og_ref[...], wgb_ref[...], preferred_element_type=f32)
    acc += jax.nn.sigmoid(g1_ref[...].astype(f32)) * jnp.dot(os_ref[...], wsb_ref[...], preferred_element_type=f32)
    acc += jax.nn.sigmoid(g2_ref[...].astype(f32)) * jnp.dot(od_ref[...], wdb_ref[...], preferred_element_type=f32)
    o_ref[...] = acc.astype(o_ref.dtype)


def _merge(og, osg, od, wg, ws, wd, proj_b, layer, tm=1024, tn=MM_TN):
    m = og.shape[0]
    n = wg.shape[2]

    def act(a):
        return pl.BlockSpec((tm, a.shape[1]), lambda j, i: (i, 0))

    def wgt(w):
        return pl.BlockSpec((None, w.shape[1], tn), lambda j, i: (layer, 0, j))

    def gate(idx):
        base = (_B_GATES + idx * D_MODEL) // tn
        return pl.BlockSpec((tm, tn), lambda j, i: (i, base + j))

    kin = og.shape[1] + osg.shape[1] + od.shape[1]
    need = 2 * (tm * kin * 2 + kin * tn * 4 + 4 * tm * tn * 2) + kin * tn * 2 + 3 * tm * tn * 4 + (4 << 20)
    return pl.pallas_call(
        _merge_kernel,
        grid=(n // tn, m // tm),
        in_specs=[act(og), act(osg), act(od), wgt(wg), wgt(ws), wgt(wd), gate(0), gate(1), gate(2)],
        out_specs=pl.BlockSpec((tm, tn), lambda j, i: (i, j)),
        out_shape=jax.ShapeDtypeStruct((m, n), jnp.bfloat16),
        scratch_shapes=[pltpu.VMEM((w.shape[1], tn), jnp.bfloat16) for w in (wg, ws, wd)],
        compiler_params=_cparams(("arbitrary", "arbitrary"), need),
        name="merge",
    )(og, osg, od, wg, ws, wd, proj_b, proj_b, proj_b)


def _ffn_up_kernel(x_ref, wa_ref, wu_ref, cwa_ref, cwu_ref, cba_ref, cbu_ref, o_ref, wb_ref, sa_ref, su_ref,
                   *, n_row_blocks, blocks_per_seq):
    i = pl.program_id(1)
    tm = x_ref.shape[0]
    tc = wa_ref.shape[1]
    H = CONV_HALO
    stash = ((sa_ref, slice(0, tc)), (su_ref, slice(tc, 2 * tc)))

    def up_and_stash(first):
        up = lax.dot_general(x_ref[...], wb_ref[...], _NN, preferred_element_type=jnp.float32)
        for s_ref, cols in stash:
            if first:
                s_ref[0:H, :] = jnp.zeros((H, tc), jnp.float32)
            else:
                s_ref[0:H, :] = jnp.where((i % blocks_per_seq) == 0, 0.0, s_ref[tm:tm + H, :])
            s_ref[H:, :] = up[:, cols]

    def conv(s_ref, w_ref, b_ref):
        full = s_ref[...]
        x1 = pltpu.roll(full, 1, axis=0)[H:]
        x2 = pltpu.roll(full, 2, axis=0)[H:]
        return b_ref[...] + w_ref[0:1, :] * x2 + w_ref[1:2, :] * x1 + w_ref[2:3, :] * full[H:]

    def emit():
        a = conv(sa_ref, cwa_ref, cba_ref)
        u = conv(su_ref, cwu_ref, cbu_ref)
        o_ref[...] = (_gelu_tanh(a) * u).astype(o_ref.dtype)

    @pl.when(i == 0)
    def _():
        for r in range(0, wa_ref.shape[0], CAST_CHUNK):
            rows = slice(r, r + CAST_CHUNK)
            wb_ref[rows, 0:tc] = wa_ref[rows, :].astype(wb_ref.dtype)
            wb_ref[rows, tc:2 * tc] = wu_ref[rows, :].astype(wb_ref.dtype)
        up_and_stash(True)

    @pl.when((i > 0) & (i < n_row_blocks))
    def _():
        emit()
        up_and_stash(False)

    @pl.when(i == n_row_blocks)
    def _():
        emit()


def _ffn_up_gate(h, w_up, conv_w, conv_b, layer, seq, tm=1024, tc=256):
    m, k = h.shape
    nj = D_FF // tc
    ni = m // tm
    cb = conv_b.reshape(conv_b.shape[0], 1, -1)

    def wspec(off, rows):
        return pl.BlockSpec((None, rows, tc), lambda j, i: (layer, 0, off + j))

    kernel = functools.partial(_ffn_up_kernel, n_row_blocks=ni, blocks_per_seq=seq // tm)
    need = 2 * (tm * k * 2 + 2 * k * tc * 4 + tm * tc * 2) + k * 2 * tc * 2 + 2 * (tm + CONV_HALO) * tc * 4 \
        + 4 * tm * 2 * tc * 4 + (4 << 20)
    return pl.pallas_call(
        kernel,
        grid=(nj, ni + 1),
        in_specs=[pl.BlockSpec((tm, k), lambda j, i: (jnp.minimum(i, ni - 1), 0)),
                  wspec(0, k), wspec(nj, k), wspec(0, CONV_WIDTH), wspec(nj, CONV_WIDTH), wspec(0, 1), wspec(nj, 1)],
        out_specs=pl.BlockSpec((tm, tc), lambda j, i: (jnp.maximum(i - 1, 0), j)),
        out_shape=jax.ShapeDtypeStruct((m, D_FF), jnp.bfloat16),
        scratch_shapes=[pltpu.VMEM((k, 2 * tc), jnp.bfloat16),
                        pltpu.VMEM((tm + CONV_HALO, tc), jnp.float32),
                        pltpu.VMEM((tm + CONV_HALO, tc), jnp.float32)],
        compiler_params=_cparams(("arbitrary", "arbitrary"), need),
        name="ffn_up_gate",
    )(h, w_up, w_up, conv_w, conv_w, cb, cb)


def _alibi_slopes(n):
    def pow2(mm):
        start = 2.0 ** (-8.0 / mm)
        return [start ** (i + 1) for i in range(mm)]
    if math.log2(n).is_integer():
        return pow2(n)
    p = 2 ** int(math.floor(math.log2(n)))
    return pow2(p) + pow2(2 * p)[0::2][: n - p]


def kernel(x, g_pre_mix, w_in, w_gla_lr, b_gla_lr, gla_norm_g, sg_ln_g, sg_ln_b, sg_w_s, sg_b_s,
           diff_lambda_q1, diff_lambda_k1, diff_lambda_q2, diff_lambda_k2, diff_norm_g,
           w_br_gla, w_br_sg, w_br_diff, w_o, g_post_mix, g_pre_ffn, w_up, conv_w, conv_b, w_down, g_post_ffn):
    B, S, D = x.shape
    M = B * S
    bf16 = jnp.bfloat16
    slopes = jnp.asarray(_alibi_slopes(DIFF_HEADS), jnp.float32)

    w_in_t = jnp.swapaxes(w_in, 1, 2)

    xf = x.reshape(M, D)
    h = _prenorm(xf, g_pre_mix[0])
    for l in range(DEPTH):
        proj_a = _matmul(h, w_in_t, l, _R_GQ, PA_COLS, 1024, bf16, "in_proj_a", transposed=True)
        proj_b = _matmul(h, w_in_t, l, _R_SU, PB_COLS, 1024, bf16, "in_proj_b", transposed=True)

        w_lr = jnp.pad(w_gla_lr[l], ((0, V7X_LANES - GLA_RANK), (0, 0)))
        log_a = _loggate(h, w_in_t, l, w_lr, b_gla_lr[l])

        o_gla = _gla(proj_a, log_a, gla_norm_g[l], B, S)
        o_sg = _spatial_gating(proj_b, sg_ln_g[l], sg_ln_b[l], sg_w_s[l], sg_b_s[l])
        lam_init = 0.8 - 0.6 * math.exp(-0.3 * l)
        o_df = _diff_attention(proj_b, slopes, diff_lambda_q1[l], diff_lambda_k1[l], diff_lambda_q2[l],
                               diff_lambda_k2[l], diff_norm_g[l], lam_init, B, S)

        merged = _merge(o_gla, o_sg, o_df, w_br_gla, w_br_sg, w_br_diff, proj_b, l)
        y = _matmul(merged, w_o, l, 0, D, 1024, bf16, "out_proj")
        xf, h = _postnorm(xf, y, g_post_mix[l], g_pre_ffn[l])

        act = _ffn_up_gate(h, w_up, conv_w, conv_b, l, S)
        f =_matmul(act, w_down, l, 0, D, 512, bf16, "ffn_down")
        g_next = g_pre_mix[l + 1] if l + 1 < DEPTH else None
        xf, h = _postnorm(xf, f, g_post_ffn[l], g_next)
    return xf.reshape(B, S, D)
```

```python
import functools
import math

import jax
import jax.numpy as jnp
from jax import lax
from jax.experimental import pallas as pl
from jax.experimental.pallas import tpu as pltpu

D_MODEL = 4096
DEPTH = 4
GLA_HEADS, GLA_DK, GLA_DV, GLA_RANK, GLA_TAU = 8, 128, 192, 16, 16.0
SG_GROUPS, SG_GROUP_WIDTH, SG_CHUNK = 8, 128, 128
DIFF_HEADS, DIFF_DK, DIFF_DV = 6, 128, 256
D_FF = 6144
CONV_WIDTH = 3
N_BRANCH = 3
EPS = 1e-6

GLA_QK = GLA_HEADS * GLA_DK
GLA_V = GLA_HEADS * GLA_DV
SG_W = SG_GROUPS * SG_GROUP_WIDTH
DIFF_QK = DIFF_HEADS * 2 * DIFF_DK
DIFF_V = DIFF_HEADS * DIFF_DV
GATES_W = N_BRANCH * D_MODEL

_R_GQ = 0
_R_GK = _R_GQ + GLA_QK
_R_GV = _R_GK + GLA_QK
_R_GG = _R_GV + GLA_V
_R_LR = _R_GG + GLA_V
_R_SU = _R_LR + GLA_RANK
IN_COLS = _R_SU + 2 * SG_W + 2 * DIFF_QK + DIFF_V + GATES_W

_A_GQ, _A_GK, _A_GV, _A_GG = _R_GQ, _R_GK, _R_GV, _R_GG
_B_SU = _R_LR
_B_SV = _B_SU + SG_W
_B_DQ = _B_SV + SG_W
_B_DK = _B_DQ + DIFF_QK
_B_DV = _B_DK + DIFF_QK
_B_GATES = _B_DV + DIFF_V
PROJ_COLS = _B_GATES + GATES_W

V7X_LANES = 128
V7X_VMEM_BYTES = 64 * 1024 * 1024
VMEM_LIMIT_CAP = V7X_VMEM_BYTES - 8 * 1024 * 1024

MM_TN = 512
GLA_CHUNK = 128
GLA_ROWS = 512
GLA_SAFE_RANGE = 80.0
ATTN_TQ = 256
SG_ROWS = 512
CONV_HALO = 8
NEG_BIG = -1e30


def _vmem_limit(nbytes):
    return int(min(VMEM_LIMIT_CAP, max(32 * 1024 * 1024, nbytes)))


def _cparams(sem, nbytes):
    return pltpu.CompilerParams(dimension_semantics=sem, vmem_limit_bytes=_vmem_limit(nbytes))


CAST_CHUNK = 1024
BF16_ROWS = 16

_NN = (((1,), (0,)), ((), ()))
_NT = (((1,), (1,)), ((), ()))


def _mm_kernel(x_ref, w_ref, o_ref, wb_ref):
    @pl.when(pl.program_id(1) == 0)
    def _():
        for r in range(0, w_ref.shape[0], CAST_CHUNK):
            wb_ref[r:r + CAST_CHUNK, :] = w_ref[r:r + CAST_CHUNK, :].astype(wb_ref.dtype)

    o_ref[...] = lax.dot_general(x_ref[...], wb_ref[...], _NN, preferred_element_type=jnp.float32).astype(o_ref.dtype)


def _matmul(x, w, layer, n, tm, out_dtype, name, tn=MM_TN):
    m, k = x.shape
    osz = jnp.dtype(out_dtype).itemsize
    need = 2 * (tm * k * 2 + k * tn * 4 + tm * tn * osz) + k * tn * 2 + tm * tn * 4 + (6 << 20)
    return pl.pallas_call(
        _mm_kernel,
        grid=(n // tn, m // tm),
        in_specs=[pl.BlockSpec((tm, k), lambda j, i: (i, 0)),
                  pl.BlockSpec((None, k, tn), lambda j, i: (layer, 0, j))],
        out_specs=pl.BlockSpec((tm, tn), lambda j, i: (i, j)),
        out_shape=jax.ShapeDtypeStruct((m, n), out_dtype),
        scratch_shapes=[pltpu.VMEM((k, tn), jnp.bfloat16)],
        compiler_params=_cparams(("arbitrary", "arbitrary"), need),
        name=name,
    )(x, w)


def _mm_stream_kernel(x_ref, w_hbm, o_ref, stage_ref, wb_ref, sem, *, layer, transposed, n_blocks, last, gap_at, gap):
    j = pl.program_id(0)
    i = pl.program_id(1)
    tn = o_ref.shape[1]
    k = x_ref.shape[1]

    def copy(jj, rows):
        if transposed:
            row0 = pl.multiple_of(jj * tn + jnp.where(jj * tn >= gap_at, gap, 0), BF16_ROWS)
            return pltpu.make_async_copy(w_hbm.at[layer, pl.ds(row0, rows), :], stage_ref.at[pl.ds(0, rows), :], sem)
        return pltpu.make_async_copy(w_hbm.at[layer, :, pl.ds(pl.multiple_of(jj * tn, tn), tn)], stage_ref, sem)

    def for_block(jj, fn):
        if last == tn:
            fn(copy(jj, tn))
        else:
            @pl.when(jj < n_blocks - 1)
            def _():
                fn(copy(jj, tn))

            @pl.when(jj == n_blocks - 1)
            def _():
                fn(copy(jj, last))

    @pl.when(i == 0)
    def _():
        @pl.when(j == 0)
        def _():
            for_block(j, lambda c: c.start())
        for_block(j, lambda c: c.wait())
        if transposed:
            for c in range(0, k, CAST_CHUNK):
                wb_ref[:, c:c + CAST_CHUNK] = stage_ref[:, c:c + CAST_CHUNK].astype(wb_ref.dtype)
        else:
            for r in range(0, k, CAST_CHUNK):
                wb_ref[r:r + CAST_CHUNK, :] = stage_ref[r:r + CAST_CHUNK, :].astype(wb_ref.dtype)

    @pl.when((i == 1) & (j + 1 < n_blocks))
    def _():
        for_block(j + 1, lambda c: c.start())

    dims = _NT if transposed else _NN
    o_ref[...] = lax.dot_general(x_ref[...], wb_ref[...], dims, preferred_element_type=jnp.float32).astype(o_ref.dtype)


def _matmul_stream(x, w, layer, n, tm, out_dtype, name, tn=1024, transposed=False, gap_at=0, gap=0):
    m, k = x.shape
    assert m // tm >= 2
    n_blocks = pl.cdiv(n, tn)
    last = n - (n_blocks - 1) * tn
    assert last == tn or (transposed and n_blocks > 1 and last % BF16_ROWS == 0)
    osz = jnp.dtype(out_dtype).itemsize
    need = 2 * (tm * k * 2 + tm * tn * osz) + k * tn * 6 + 2 * tm * tn * 4 + (4 << 20)
    wshape = (tn, k) if transposed else (k, tn)
    kernel = functools.partial(_mm_stream_kernel, layer=layer, transposed=transposed, n_blocks=n_blocks, last=last,
                               gap_at=gap_at, gap=gap)
    return pl.pallas_call(
        kernel,
        grid=(n_blocks, m // tm),
        in_specs=[pl.BlockSpec((tm, k), lambda j, i: (i, 0)), pl.BlockSpec(memory_space=pl.ANY)],
        out_specs=pl.BlockSpec((tm, tn), lambda j, i: (i, j)),
        out_shape=jax.ShapeDtypeStruct((m, n), out_dtype),
        scratch_shapes=[pltpu.VMEM(wshape, jnp.float32), pltpu.VMEM(wshape, jnp.bfloat16),
                        pltpu.SemaphoreType.DMA(())],
        compiler_params=_cparams(("arbitrary", "arbitrary"), need),
        name=name,
    )(x, w)


def _sigmoid(x):
    return 0.5 * jnp.tanh(0.5 * x) + 0.5


def _rms(xf, g):
    return xf * lax.rsqrt(jnp.mean(xf * xf, axis=-1, keepdims=True) + EPS) * g


def _prenorm_kernel(x_ref, g_ref, h_ref):
    h_ref[...] = _rms(x_ref[...], g_ref[...]).astype(h_ref.dtype)


def _prenorm(x, g, tm=256):
    m, d = x.shape
    return pl.pallas_call(
        _prenorm_kernel,
        grid=(m // tm,),
        in_specs=[pl.BlockSpec((tm, d), lambda i: (i, 0)), pl.BlockSpec((1, d), lambda i: (0, 0))],
        out_specs=pl.BlockSpec((tm, d), lambda i: (i, 0)),
        out_shape=jax.ShapeDtypeStruct((m, d), jnp.bfloat16),
        compiler_params=_cparams(("parallel",), 2 * tm * d * 6 + (8 << 20)),
        name="prenorm",
    )(x, g.reshape(1, d))


def _postnorm_kernel(x_ref, y_ref, gp_ref, gn_ref, xo_ref, h_ref):
    xn = x_ref[...] + _rms(y_ref[...].astype(jnp.float32), gp_ref[...])
    xo_ref[...] = xn
    h_ref[...] = _rms(xn, gn_ref[...]).astype(h_ref.dtype)


def _postnorm_last_kernel(x_ref, y_ref, gp_ref, xo_ref):
    xo_ref[...] = x_ref[...] + _rms(y_ref[...].astype(jnp.float32), gp_ref[...])


def _postnorm(x, y, g_post, g_next, tm=256):
    m, d = x.shape
    row = pl.BlockSpec((tm, d), lambda i: (i, 0))
    vec = pl.BlockSpec((1, d), lambda i: (0, 0))
    params = _cparams(("parallel",), 2 * tm * d * 12 + (8 << 20))
    if g_next is None:
        return pl.pallas_call(
            _postnorm_last_kernel, grid=(m // tm,), in_specs=[row, row, vec], out_specs=row,
            out_shape=jax.ShapeDtypeStruct((m, d), jnp.float32), compiler_params=params, name="postnorm_last",
        )(x, y, g_post.reshape(1, d)), None
    return pl.pallas_call(
        _postnorm_kernel, grid=(m // tm,), in_specs=[row, row, vec, vec], out_specs=[row, row],
        out_shape=[jax.ShapeDtypeStruct((m, d), jnp.float32), jax.ShapeDtypeStruct((m, d), jnp.bfloat16)],
        compiler_params=params, name="postnorm",
    )(x, y, g_post.reshape(1, d), g_next.reshape(1, d))


def _loggate_kernel(h_ref, w1_ref, w2_ref, b_ref, o_ref, w1b_ref):
    @pl.when(pl.program_id(0) == 0)
    def _():
        w1b_ref[...] = w1_ref[...].astype(w1b_ref.dtype)

    glr = lax.dot_general(h_ref[...], w1b_ref[...], _NT, preferred_element_type=jnp.float32)
    z = jnp.dot(glr, w2_ref[...], preferred_element_type=jnp.float32,
                precision=lax.Precision.HIGHEST) + b_ref[...]
    o_ref[...] = (jnp.minimum(z, 0.0) - jnp.log1p(jnp.exp(-jnp.abs(z)))) * (1.0 / GLA_TAU)


def _loggate(h, w_in_t, layer, w2, b, tm=1024):
    m, d = h.shape
    r, n = w2.shape
    return pl.pallas_call(
        _loggate_kernel,
        grid=(m // tm,),
        in_specs=[pl.BlockSpec((tm, d), lambda i: (i, 0)),
                  pl.BlockSpec((None, r, d), lambda i: (layer, _R_LR // r, 0)),
                  pl.BlockSpec((r, n), lambda i: (0, 0)), pl.BlockSpec((1, n), lambda i: (0, 0))],
        out_specs=pl.BlockSpec((tm, n), lambda i: (i, 0)),
        out_shape=jax.ShapeDtypeStruct((m, n), jnp.float32),
        scratch_shapes=[pltpu.VMEM((r, d), jnp.bfloat16)],
        compiler_params=_cparams(("arbitrary",), 2 * (tm * d * 2 + tm * n * 4) + (24 << 20)),
        name="loggate",
    )(h, w_in_t, w2, b.reshape(1, n))


def _gla_kernel(q_ref, k_ref, v0_ref, v1_ref, v2_ref, g0_ref, g1_ref, g2_ref, la_ref, na_ref, nb_ref,
                o_ref, state_ref, b_ref, intra_ref, kf_ref, vf_ref):
    C = GLA_CHUNK
    DK = GLA_DK
    L = V7X_LANES
    n_chunks = q_ref.shape[0] // C
    half = GLA_DV - L

    @pl.when(pl.program_id(2) == 0)
    def _():
        state_ref[...] = jnp.zeros_like(state_ref)

    row = lax.broadcasted_iota(jnp.int32, (C, C), 0)
    col = lax.broadcasted_iota(jnp.int32, (C, C), 1)
    causal = row >= col
    tril = causal.astype(jnp.bfloat16)
    eye = (row == col).astype(jnp.float32)
    lane2 = lax.broadcasted_iota(jnp.int32, (C, 2 * L), 1)
    lane1 = lax.broadcasted_iota(jnp.int32, (C, L), 1)
    row_id = lax.broadcasted_iota(jnp.int32, (C, 1), 0)
    scale = DK ** -0.5
    own_v = (v0_ref, v2_ref)
    own_g = (g0_ref, g2_ref)
    gains = (na_ref, nb_ref)
    valid = (lane2 < L + half, (lane2 < L) | (lane2 >= L + half))

    for c in range(n_chunks):
        rows = slice(c * C, (c + 1) * C)
        la = la_ref[rows, :]
        la_hi = la.astype(jnp.bfloat16)
        la_lo = (la - la_hi.astype(jnp.float32)).astype(jnp.bfloat16)
        b_ref[rows, :] = (jnp.dot(tril, la_hi, preferred_element_type=jnp.float32)
                          + jnp.dot(tril, la_lo, preferred_element_type=jnp.float32))
    b_end = b_ref[C - 1:C, :]
    for c in range(1, n_chunks):
        b_end = jnp.minimum(b_end, b_ref[(c + 1) * C - 1:(c + 1) * C, :])
    tame = jnp.max(-b_end) <= GLA_SAFE_RANGE

    def load_qkv(rows, hh):
        sl = slice(hh * DK, (hh + 1) * DK)
        q = q_ref[rows, sl].astype(jnp.float32) * scale
        k = k_ref[rows, sl].astype(jnp.float32)
        v = jnp.concatenate([own_v[hh][rows, :], v1_ref[rows, :]], axis=1)
        return q, k, v

    @pl.when(tame)
    def _():
        for c in range(n_chunks):
            rows = slice(c * C, (c + 1) * C)
            mid = c * C + C // 2 - 1
            for hh in range(2):
                sl = slice(hh * DK, (hh + 1) * DK)
                q, k, v = load_qkv(rows, hh)
                bh = b_ref[rows, sl]
                b_mid = b_ref[mid:mid + 1, sl]
                qr = (q * jnp.exp(bh - b_mid)).astype(jnp.bfloat16)
                kr = (k * jnp.exp(b_mid - bh)).astype(jnp.bfloat16)
                a = lax.dot_general(qr, kr, (((1,), (1,)), ((), ())), preferred_element_type=jnp.float32)
                a = jnp.where(causal, a, 0.0).astype(jnp.bfloat16)
                intra_ref[hh, rows, :] = jnp.dot(a, v, preferred_element_type=jnp.float32)

    @pl.when(jnp.logical_not(tame))
    def _():
        def chunk(c, carry):
            r0 = pl.multiple_of(c * C, C)
            rows = pl.ds(r0, C)
            for hh in range(2):
                sl = slice(hh * DK, (hh + 1) * DK)
                q, k, v = load_qkv(rows, hh)
                bh = b_ref[rows, sl]
                kf_ref[...] = k
                vf_ref[...] = v.astype(jnp.float32)

                def src(grp, acc):
                    s0 = pl.multiple_of(grp * 8, 8)
                    b8 = b_ref[pl.ds(r0 + s0, 8), sl]
                    k8 = kf_ref[pl.ds(s0, 8), :]
                    v8 = vf_ref[pl.ds(s0, 8), :]
                    for j in range(8):
                        p = q * jnp.exp(jnp.minimum(bh - b8[j:j + 1], 0.0)) * k8[j:j + 1]
                        a_s = jnp.sum(p, axis=-1, keepdims=True)
                        a_s = jnp.where(row_id >= s0 + j, a_s, 0.0)
                        acc = acc + a_s * v8[j:j + 1]
                    return acc
                intra_ref[hh, rows, :] = lax.fori_loop(0, C // 8, src, jnp.zeros((C, 2 * L), jnp.float32))
            return carry
        lax.fori_loop(0, n_chunks, chunk, 0)

    for c in range(n_chunks):
        rows = slice(c * C, (c + 1) * C)
        last = (c + 1) * C - 1
        y_shared = []
        for hh in range(2):
            sl = slice(hh * DK, (hh + 1) * DK)
            q, k, v = load_qkv(rows, hh)
            bh = b_ref[rows, sl]
            b_last = b_ref[last:last + 1, sl]
            st = state_ref[hh]
            qb = (q * jnp.exp(bh)).astype(jnp.bfloat16)
            o = intra_ref[hh, rows, :] + jnp.dot(qb, st.astype(jnp.bfloat16), preferred_element_type=jnp.float32)
            kd = (k * jnp.exp(b_last - bh)).astype(jnp.bfloat16)
            dec_col = jnp.sum(eye * jnp.exp(b_last), axis=-1, keepdims=True)
            state_ref[hh] = st * dec_col + lax.dot_general(
                kd, v, (((0,), (0,)), ((), ())), preferred_element_type=jnp.float32)

            ss = jnp.sum(jnp.where(valid[hh], o * o, 0.0), axis=-1, keepdims=True) * (1.0 / GLA_DV)
            g = jnp.concatenate([own_g[hh][rows, :], g1_ref[rows, :]], axis=1).astype(jnp.float32)
            y = (o * lax.rsqrt(ss + EPS) * gains[hh][...]) * (g * _sigmoid(g))
            o_ref[rows, 2 * L * hh:2 * L * hh + L] = y[:, :L].astype(o_ref.dtype)
            y_shared.append(y[:, L:])
        o_ref[rows, L:2 * L] = jnp.where(lane1 < half, y_shared[0], y_shared[1]).astype(o_ref.dtype)


def _gla(proj_a, la, norm_g, batch, seq):
    m = proj_a.shape[0]
    R = GLA_ROWS
    nr = seq // R
    L = V7X_LANES
    W2 = 2 * L
    half = GLA_DV - L
    pad = jnp.zeros((W2 - GLA_DV,), norm_g.dtype)
    gain_a = jnp.concatenate([norm_g, pad]).reshape(1, W2)
    gain_b = jnp.concatenate([norm_g[half:], pad, norm_g[:half]]).reshape(1, W2)

    def pair(off):
        base = off // W2
        return pl.BlockSpec((R, W2), lambda b, p, r: (b * nr + r, base + p))

    def tile(off, t):
        base = off // L + t
        return pl.BlockSpec((R, L), lambda b, p, r: (b * nr + r, base + 3 * p))

    const = pl.BlockSpec((1, W2), lambda b, p, r: (0, 0))
    in_specs = [pair(_A_GQ), pair(_A_GK),
                tile(_A_GV, 0), tile(_A_GV, 1), tile(_A_GV, 2),
                tile(_A_GG, 0), tile(_A_GG, 1), tile(_A_GG, 2),
                pl.BlockSpec((R, W2), lambda b, p, r: (b * nr + r, p)), const, const]
    return pl.pallas_call(
        _gla_kernel,
        grid=(batch, GLA_HEADS // 2, nr),
        in_specs=in_specs,
        out_specs=pl.BlockSpec((R, 3 * L), lambda b, p, r: (b * nr + r, p)),
        out_shape=jax.ShapeDtypeStruct((m, GLA_V), jnp.bfloat16),
        scratch_shapes=[pltpu.VMEM((2, GLA_DK, W2), jnp.float32),
                        pltpu.VMEM((R, W2), jnp.float32),
                        pltpu.VMEM((2, R, W2), jnp.float32),
                        pltpu.VMEM((GLA_CHUNK, GLA_DK), jnp.float32),
                        pltpu.VMEM((GLA_CHUNK, W2), jnp.float32)],
        compiler_params=_cparams(("parallel", "parallel", "arbitrary"), 32 << 20),
        name="gla",
    )(proj_a, proj_a, proj_a, proj_a, proj_a, proj_a, proj_a, proj_a, la, gain_a, gain_b)


def _gelu_tanh(x):
    return 0.5 * x * (1.0 + jnp.tanh(math.sqrt(2.0 / math.pi) * (x + 0.044715 * (x * x * x))))


def _sg_kernel(u_ref, v_ref, lg_ref, lb_ref, w_ref, bs_ref, o_ref):
    C = SG_CHUNK
    GW = SG_GROUP_WIDTH
    row = lax.broadcasted_iota(jnp.int32, (C, C), 0)
    col = lax.broadcasted_iota(jnp.int32, (C, C), 1)
    causal = row >= col
    for c in range(u_ref.shape[0] // C):
        rows = slice(c * C, (c + 1) * C)
        v = _gelu_tanh(v_ref[rows, :].astype(jnp.float32))
        mu = jnp.mean(v, axis=-1, keepdims=True)
        vc = v - mu
        var = jnp.mean(vc * vc, axis=-1, keepdims=True)
        vn = (vc * lax.rsqrt(var + EPS) * lg_ref[...] + lb_ref[...]).astype(jnp.bfloat16)
        for g in range(SG_GROUPS):
            cols = slice(g * GW, (g + 1) * GW)
            w = jnp.where(causal, w_ref[g], 0.0).astype(jnp.bfloat16)
            f = jnp.dot(w, vn[:, cols], preferred_element_type=jnp.float32) + bs_ref[g]
            u = _gelu_tanh(u_ref[rows, cols].astype(jnp.float32))
            o_ref[rows, cols] = (u * f).astype(o_ref.dtype)


def _spatial_gating(proj_b, ln_g, ln_b, w_s, b_s):
    m = proj_b.shape[0]
    R = SG_ROWS
    G, C = SG_GROUPS, SG_CHUNK
    return pl.pallas_call(
        _sg_kernel,
        grid=(m // R,),
        in_specs=[pl.BlockSpec((R, SG_W), lambda i: (i, _B_SU // SG_W)),
                  pl.BlockSpec((R, SG_W), lambda i: (i, _B_SV // SG_W)),
                  pl.BlockSpec((1, SG_W), lambda i: (0, 0)),
                  pl.BlockSpec((1, SG_W), lambda i: (0, 0)),
                  pl.BlockSpec((G, C, C), lambda i: (0, 0, 0)),
                  pl.BlockSpec((G, C, 1), lambda i: (0, 0, 0))],
        out_specs=pl.BlockSpec((R, SG_W), lambda i: (i, 0)),
        out_shape=jax.ShapeDtypeStruct((m, SG_W), jnp.bfloat16),
        compiler_params=_cparams(("parallel",), 32 << 20),
        name="spatial_gating",
    )(proj_b, proj_b, ln_g.reshape(1, SG_W), ln_b.reshape(1, SG_W), w_s, b_s.reshape(G, C, 1))


def _diff_attn_kernel(slopes_ref, q_ref, k_ref, v_ref, lq1_ref, lk1_ref, lq2_ref, lk2_ref, ng_ref,
                      o_ref, bias_ref, *, lam_init):
    TQ, DK = ATTN_TQ, DIFF_DK
    seq = q_ref.shape[0]
    slope = slopes_ref[pl.program_id(1)]
    scale = DK ** -0.5
    nt = (((1,), (1,)), ((), ()))

    col = lax.broadcasted_iota(jnp.int32, (TQ, seq), 1)
    row = lax.broadcasted_iota(jnp.int32, (TQ, seq), 0)
    bias_ref[...] = (col - row).astype(jnp.float32) * slope
    tri = lax.broadcasted_iota(jnp.int32, (TQ, TQ), 1) <= lax.broadcasted_iota(jnp.int32, (TQ, TQ), 0)

    lam = (jnp.exp(jnp.sum(lq1_ref[...] * lk1_ref[...])) - jnp.exp(jnp.sum(lq2_ref[...] * lk2_ref[...]))
           + lam_init)

    for qi in range(seq // TQ):
        q0 = qi * TQ
        blk = slice(q0, q0 + TQ)
        outs = []
        for mp in range(2):
            sl = slice(mp * DK, (mp + 1) * DK)
            q = q_ref[blk, sl]
            s_d = lax.dot_general(q, k_ref[blk, sl], nt, preferred_element_type=jnp.float32) * scale + bias_ref[:, blk]
            s_d = jnp.where(tri, s_d, NEG_BIG)
            m = jnp.max(s_d, axis=-1, keepdims=True)
            if qi > 0:
                s_p = (lax.dot_general(q, k_ref[0:q0, sl], nt, preferred_element_type=jnp.float32) * scale
                       + bias_ref[:, 0:q0])
                m = jnp.maximum(m, jnp.max(s_p, axis=-1, keepdims=True))
            p_d = jnp.exp(s_d - m)
            l = jnp.sum(p_d, axis=-1, keepdims=True)
            acc = jnp.dot(p_d.astype(jnp.bfloat16), v_ref[blk, :], preferred_element_type=jnp.float32)
            if qi > 0:
                p_p = jnp.exp(s_p - m)
                l = l + jnp.sum(p_p, axis=-1, keepdims=True)
                acc = acc + jnp.dot(p_p.astype(jnp.bfloat16), v_ref[0:q0, :], preferred_element_type=jnp.float32)
            outs.append(acc / l)
        o = outs[0] - lam * outs[1]
        o_ref[blk, :] = (_rms(o, ng_ref[...]) * (1.0 - lam_init)).astype(o_ref.dtype)


def _diff_attention(proj_b, slopes, lq1, lk1, lq2, lk2, norm_g, lam_init, batch, seq):
    m = proj_b.shape[0]
    W = 2 * DIFF_DK
    vec = pl.BlockSpec((1, DIFF_DK), lambda b, h: (0, 0))
    kernel = functools.partial(_diff_attn_kernel, lam_init=lam_init)
    return pl.pallas_call(
        kernel,
        grid=(batch, DIFF_HEADS),
        in_specs=[pl.BlockSpec(memory_space=pltpu.SMEM),
                  pl.BlockSpec((seq, W), lambda b, h: (b, _B_DQ // W + h)),
                  pl.BlockSpec((seq, W), lambda b, h: (b, _B_DK // W + h)),
                  pl.BlockSpec((seq, DIFF_DV), lambda b, h: (b, _B_DV // DIFF_DV + h)),
                  vec, vec, vec, vec,
                  pl.BlockSpec((1, DIFF_DV), lambda b, h: (0, 0))],
        out_specs=pl.BlockSpec((seq, DIFF_DV), lambda b, h: (b, h)),
        out_shape=jax.ShapeDtypeStruct((m, DIFF_V), jnp.bfloat16),
        scratch_shapes=[pltpu.VMEM((ATTN_TQ, seq), jnp.float32)],
        compiler_params=_cparams(("parallel", "arbitrary"), 48 << 20),
        name="diff_attention",
    )(slopes, proj_b, proj_b, proj_b, lq1.reshape(1, -1), lk1.reshape(1, -1), lq2.reshape(1, -1),
      lk2.reshape(1, -1), norm_g.reshape(1, -1))


def _merge_kernel(og_ref, os_ref, od_ref, wg_ref, ws_ref, wd_ref, g0_ref, g1_ref, g2_ref, o_ref,
                  wgb_ref, wsb_ref, wdb_ref):
    f32 = jnp.float32

    @pl.when(pl.program_id(1) == 0)
    def _():
        wgb_ref[...] = wg_ref[...].astype(wgb_ref.dtype)
        wsb_ref[...] = ws_ref[...].astype(wsb_ref.dtype)
        wdb_ref[...] = wd_ref[...].astype(wdb_ref.dtype)

    acc = _sigmoid(g0_ref[...].astype(f32)) * jnp.dot(og_ref[...], wgb_ref[...], preferred_element_type=f32)
    acc += _sigmoid(g1_ref[...].astype(f32)) * jnp.dot(os_ref[...], wsb_ref[...], preferred_element_type=f32)
    acc += _sigmoid(g2_ref[...].astype(f32)) * jnp.dot(od_ref[...], wdb_ref[...], preferred_element_type=f32)
    o_ref[...] = acc.astype(o_ref.dtype)


def _merge(og, osg, od, wg, ws, wd, proj_b, layer, tm=1024, tn=MM_TN):
    m = og.shape[0]
    n = wg.shape[2]

    def act(a):
        return pl.BlockSpec((tm, a.shape[1]), lambda j, i: (i, 0))

    def wgt(w):
        return pl.BlockSpec((None, w.shape[1], tn), lambda j, i: (layer, 0, j))

    def gate(idx):
        base = (_B_GATES + idx * D_MODEL) // tn
        return pl.BlockSpec((tm, tn), lambda j, i: (i, base + j))

    kin = og.shape[1] + osg.shape[1] + od.shape[1]
    need = 2 * (tm * kin * 2 + kin * tn * 4 + 4 * tm * tn * 2) + kin * tn * 2 + 3 * tm * tn * 4 + (4 << 20)
    return pl.pallas_call(
        _merge_kernel,
        grid=(n // tn, m // tm),
        in_specs=[act(og), act(osg), act(od), wgt(wg), wgt(ws), wgt(wd), gate(0), gate(1), gate(2)],
        out_specs=pl.BlockSpec((tm, tn), lambda j, i: (i, j)),
        out_shape=jax.ShapeDtypeStruct((m, n), jnp.bfloat16),
        scratch_shapes=[pltpu.VMEM((w.shape[1], tn), jnp.bfloat16) for w in (wg, ws, wd)],
        compiler_params=_cparams(("arbitrary", "arbitrary"), need),
        name="merge",
    )(og, osg, od, wg, ws, wd, proj_b, proj_b, proj_b)


def _ffn_up_kernel(x_ref, wa_ref, wu_ref, cwa_ref, cwu_ref, cba_ref, cbu_ref, o_ref, wb_ref, sa_ref, su_ref,
                   *, n_row_blocks, blocks_per_seq):
    i = pl.program_id(1)
    tm = x_ref.shape[0]
    tc = wa_ref.shape[1]
    H = CONV_HALO
    stash = ((sa_ref, slice(0, tc)), (su_ref, slice(tc, 2 * tc)))

    def up_and_stash(first):
        up = lax.dot_general(x_ref[...], wb_ref[...], _NN, preferred_element_type=jnp.float32)
        for s_ref, cols in stash:
            if first:
                s_ref[0:H, :] = jnp.zeros((H, tc), jnp.float32)
            else:
                s_ref[0:H, :] = jnp.where((i % blocks_per_seq) == 0, 0.0, s_ref[tm:tm + H, :])
            s_ref[H:, :] = up[:, cols]

    def conv(s_ref, w_ref, b_ref):
        full = s_ref[...]
        x1 = pltpu.roll(full, 1, axis=0)[H:]
        x2 = pltpu.roll(full, 2, axis=0)[H:]
        return b_ref[...] + w_ref[0:1, :] * x2 + w_ref[1:2, :] * x1 + w_ref[2:3, :] * full[H:]

    def emit():
        a = conv(sa_ref, cwa_ref, cba_ref)
        u = conv(su_ref, cwu_ref, cbu_ref)
        o_ref[...] = (_gelu_tanh(a) * u).astype(o_ref.dtype)

    @pl.when(i == 0)
    def _():
        for r in range(0, wa_ref.shape[0], CAST_CHUNK):
            rows = slice(r, r + CAST_CHUNK)
            wb_ref[rows, 0:tc] = wa_ref[rows, :].astype(wb_ref.dtype)
            wb_ref[rows, tc:2 * tc] = wu_ref[rows, :].astype(wb_ref.dtype)
        up_and_stash(True)

    @pl.when((i > 0) & (i < n_row_blocks))
    def _():
        emit()
        up_and_stash(False)

    @pl.when(i == n_row_blocks)
    def _():
        emit()


def _ffn_up_gate(h, w_up, conv_w, conv_b, layer, seq, tm=1024, tc=256):
    m, k = h.shape
    nj = D_FF // tc
    ni = m // tm
    cb = conv_b.reshape(conv_b.shape[0], 1, -1)

    def wspec(off, rows):
        return pl.BlockSpec((None, rows, tc), lambda j, i: (layer, 0, off + j))

    kernel = functools.partial(_ffn_up_kernel, n_row_blocks=ni, blocks_per_seq=seq // tm)
    need = 2 * (tm * k * 2 + 2 * k * tc * 4 + tm * tc * 2) + k * 2 * tc * 2 + 2 * (tm + CONV_HALO) * tc * 4 \
        + 4 * tm * 2 * tc * 4 + (4 << 20)
    return pl.pallas_call(
        kernel,
        grid=(nj, ni + 1),
        in_specs=[pl.BlockSpec((tm, k), lambda j, i: (jnp.minimum(i, ni - 1), 0)),
                  wspec(0, k), wspec(nj, k), wspec(0, CONV_WIDTH), wspec(nj, CONV_WIDTH), wspec(0, 1), wspec(nj, 1)],
        out_specs=pl.BlockSpec((tm, tc), lambda j, i: (jnp.maximum(i - 1, 0), j)),
        out_shape=jax.ShapeDtypeStruct((m, D_FF), jnp.bfloat16),
        scratch_shapes=[pltpu.VMEM((k, 2 * tc), jnp.bfloat16),
                        pltpu.VMEM((tm + CONV_HALO, tc), jnp.float32),
                        pltpu.VMEM((tm + CONV_HALO, tc), jnp.float32)],
        compiler_params=_cparams(("arbitrary", "arbitrary"), need),
        name="ffn_up_gate",
    )(h, w_up, w_up, conv_w, conv_w, cb, cb)


def _alibi_slopes(n):
    def pow2(mm):
        start = 2.0 ** (-8.0 / mm)
        return [start ** (i + 1) for i in range(mm)]
    if math.log2(n).is_integer():
        return pow2(n)
    p = 2 ** int(math.floor(math.log2(n)))
    return pow2(p) + pow2(2 * p)[0::2][: n - p]


def kernel(x, g_pre_mix, w_in, w_gla_lr, b_gla_lr, gla_norm_g, sg_ln_g, sg_ln_b, sg_w_s, sg_b_s,
           diff_lambda_q1, diff_lambda_k1, diff_lambda_q2, diff_lambda_k2, diff_norm_g,
           w_br_gla, w_br_sg, w_br_diff, w_o, g_post_mix, g_pre_ffn, w_up, conv_w, conv_b, w_down, g_post_ffn):
    B, S, D = x.shape
    M = B * S
    bf16 = jnp.bfloat16
    slopes = jnp.asarray(_alibi_slopes(DIFF_HEADS), jnp.float32)

    w_in_t = jnp.swapaxes(w_in, 1, 2)

    xf = x.reshape(M, D)
    h = _prenorm(xf, g_pre_mix[0])
    for l in range(DEPTH):
        proj = _matmul_stream(h, w_in_t, l, PROJ_COLS, 1024, bf16, "in_proj", transposed=True,
                              gap_at=_R_LR, gap=GLA_RANK)
        proj_a = proj_b = proj

        w_lr = jnp.pad(w_gla_lr[l], ((0, V7X_LANES - GLA_RANK), (0, 0)))
        log_a = _loggate(h, w_in_t, l, w_lr, b_gla_lr[l])

        o_gla = _gla(proj_a, log_a, gla_norm_g[l], B, S)
        o_sg = _spatial_gating(proj_b, sg_ln_g[l], sg_ln_b[l], sg_w_s[l], sg_b_s[l])
        lam_init = 0.8 - 0.6 * math.exp(-0.3 * l)
        o_df = _diff_attention(proj_b, slopes, diff_lambda_q1[l], diff_lambda_k1[l], diff_lambda_q2[l],
                               diff_lambda_k2[l], diff_norm_g[l], lam_init, B, S)

        merged = _merge(o_gla, o_sg, o_df, w_br_gla, w_br_sg, w_br_diff, proj_b, l)
        y = _matmul_stream(merged, w_o, l, D, 1024, bf16, "out_proj")
        xf, h = _postnorm(xf, y, g_post_mix[l], g_pre_ffn[l])

        act = _ffn_up_gate(h, w_up, conv_w, conv_b, l, S)
        f = _matmul(act, w_down, l, D, 512, bf16, "ffn_down")
        g_next = g_pre_mix[l + 1] if l + 1 < DEPTH else None
        xf, h = _postnorm(xf, f, g_post_ffn[l], g_next)
    return xf.reshape(B, S, D)
```
